```python
import math
import jax
import jax.numpy as jnp
from jax import lax
import numpy as np

D_MODEL = 1024
BATCH = 8
SEQ = 4096
DEPTH = 2

D_MIX = D_MODEL
N_MIXERS = 4
GROUP_W = D_MIX // N_MIXERS
HEAD_DIM = 64
A_HEADS = GROUP_W // HEAD_DIM
A_KV_HEADS = A_HEADS // 2
C_HEADS = GROUP_W // HEAD_DIM
C_SUB = HEAD_DIM // 2
HY_EMB = 33
HY_BANDS = (HY_EMB - 1) // 2
HY_FFN = 64
HY_SHIFT = 0.05
HY_FAST = 0.3
HY_SLOW = 1.5
HY_TARGET = 1e-2
CONV_W = 3
GRID_W = 64
ROPE_THETA = 10000.0
Q_BLOCK = 128
EPS = 1e-6

SPLIT_SIZES = (
    A_HEADS * HEAD_DIM, A_KV_HEADS * HEAD_DIM, A_KV_HEADS * HEAD_DIM, GROUP_W,
    3 * GROUP_W, GROUP_W,
    2 * C_HEADS * C_SUB, 2 * C_HEADS * C_SUB, C_HEADS * HEAD_DIM, GROUP_W,
    3 * GROUP_W, GROUP_W,
)
D_IN = sum(SPLIT_SIZES)
SPLIT_POINTS = tuple(sum(SPLIT_SIZES[:i + 1]) for i in range(len(SPLIT_SIZES) - 1))

kernel_name = "hymba_style_bidir_hybrid_encoder"


def rmsnorm(x, g):
    xf = x.astype(jnp.float32)
    y = xf * lax.rsqrt(jnp.mean(xf * xf, axis=-1, keepdims=True) + EPS)
    return (y * g.astype(jnp.float32)).astype(x.dtype)


def rope_cos_sin(pos, dim):
    inv = ROPE_THETA ** (-jnp.arange(0, dim, 2, dtype=jnp.float32) / dim)
    ang = pos.astype(jnp.float32)[:, None] * inv[None, :]
    return jnp.cos(ang), jnp.sin(ang)


def apply_rope(x, cs):
    cos, sin = cs
    xf = x.astype(jnp.float32)
    x1, x2 = jnp.split(xf, 2, axis=-1)
    c, s = cos[:, None, :], sin[:, None, :]
    return jnp.concatenate([x1 * c - x2 * s, x2 * c + x1 * s], axis=-1).astype(x.dtype)


def apply_axial_rope(x, cs_row, cs_col):
    xr, xc = jnp.split(x, 2, axis=-1)
    return jnp.concatenate([apply_rope(xr, cs_row), apply_rope(xc, cs_col)], axis=-1)


def dwconv3(x, w, b=None):
    xp = jnp.pad(x, ((0, 0), (1, 1), (0, 0)))
    y = xp[:, :-2] * w[0] + xp[:, 1:-1] * w[1] + xp[:, 2:] * w[2]
    return y if b is None else y + b


def to_blocks(q):
    b, s = q.shape[:2]
    q = q.reshape((b, s // Q_BLOCK, Q_BLOCK) + q.shape[2:])
    return jnp.moveaxis(q, 1, 0)


def from_blocks(o):
    o = jnp.moveaxis(o, 0, 1)
    return o.reshape((o.shape[0], o.shape[1] * o.shape[2]) + o.shape[3:])


def gqa_attention(q, k, v):
    b, s, hq, d = q.shape
    hkv = k.shape[2]
    qg = q.reshape(b, s, hkv, hq // hkv, d)
    scale = d ** -0.5

    def block(qb):
        sc = jnp.einsum('bqhgd,bshd->bhgqs', qb, k, preferred_element_type=jnp.float32) * scale
        p = jax.nn.softmax(sc, axis=-1).astype(v.dtype)
        return jnp.einsum('bhgqs,bshd->bqhgd', p, v)

    o = from_blocks(lax.map(block, to_blocks(qg)))
    return o.reshape(b, s, hq * d)


def diff_attention(q, k, v, lam):
    scale = q.shape[-1] ** -0.5

    def block(qb):
        sc = jnp.einsum('bqhcd,bshcd->bhcqs', qb, k, preferred_element_type=jnp.float32) * scale
        p = jax.nn.softmax(sc, axis=-1)
        w = (p[:, :, 0] - lam * p[:, :, 1]).astype(v.dtype)
        return jnp.einsum('bhqs,bshd->bqhd', w, v)

    return from_blocks(lax.map(block, to_blocks(q)))


def hyena_filter(L, w1, b1, freq, w2, b2, w3):
    f32 = jnp.float32
    t = jnp.linspace(0.0, 1.0, L, dtype=f32)[:, None]
    w = 2.0 * math.pi * jnp.arange(L, dtype=f32)[:, None] / L
    f = jnp.linspace(1e-4, HY_BANDS - 1, HY_BANDS, dtype=f32)[None, :]
    emb = jnp.concatenate([t, jnp.cos(f * w), -jnp.sin(f * w)], axis=-1)
    fr = freq.astype(f32)
    h = jnp.sin(fr * (emb @ w1.astype(f32) + b1.astype(f32)))
    h = jnp.sin(fr * (h @ w2.astype(f32) + b2.astype(f32)))
    h = h @ w3.astype(f32)
    max_decay = math.log(HY_TARGET) / HY_FAST
    min_decay = math.log(HY_TARGET) / HY_SLOW
    deltas = jnp.linspace(min_decay, max_decay, GROUP_W, dtype=f32)
    window = jnp.exp(-t * jnp.abs(deltas)[None, :]) + HY_SHIFT
    h_fwd = h[:, :GROUP_W] * window
    h_bwd = h[:, GROUP_W:] * window
    kern = jnp.concatenate([h_fwd, jnp.zeros((1, GROUP_W), f32), h_bwd[:0:-1]], axis=0)
    return kern / jnp.sum(jnp.abs(kern), axis=0, keepdims=True)


def bidir_fftconv(z, kern, bias):
    L = z.shape[1]
    n = 2 * L
    zf32 = z.astype(jnp.float32)
    zf = jnp.fft.rfft(zf32, n=n, axis=1)
    kf = jnp.fft.rfft(kern, n=n, axis=0)
    y = jnp.fft.irfft(zf * kf[None], n=n, axis=1)[:, :L]
    return (y + bias.astype(jnp.float32) * zf32).astype(z.dtype)


def hybrid_layer(x, c, layer_idx, cs_row, cs_col, cs_seq, norm_g, w_ada, b_ada, w_in, w_out,
                 a_qn, a_kn, hy_conv_w, hy_conv_b, hy_w1, hy_b1, hy_freq, hy_w2, hy_b2, hy_w3,
                 hy_bias, c_qn, c_kn, lam_q1, lam_k1, lam_q2, lam_k2, c_subln, sc_conv_w):
    b, s, _ = x.shape
    mod = jax.nn.silu(c) @ w_ada + b_ada
    shift, scale, gate = jnp.split(mod, 3, axis=-1)
    h = rmsnorm(x, norm_g) * (1.0 + scale[:, None, :]) + shift[:, None, :]
    proj = h @ w_in
    (a_q, a_k, a_v, a_g, b_p, b_g, c_q, c_k, c_v, c_g, d_p, d_g) = jnp.split(proj, SPLIT_POINTS, axis=-1)

    qa = apply_axial_rope(rmsnorm(a_q.reshape(b, s, A_HEADS, HEAD_DIM), a_qn), cs_row, cs_col)
    ka = apply_axial_rope(rmsnorm(a_k.reshape(b, s, A_KV_HEADS, HEAD_DIM), a_kn), cs_row, cs_col)
    va = a_v.reshape(b, s, A_KV_HEADS, HEAD_DIM)
    y_a = jax.nn.silu(a_g) * gqa_attention(qa, ka, va)

    x0, x1, vb = jnp.split(dwconv3(b_p, hy_conv_w, hy_conv_b), 3, axis=-1)
    kern = hyena_filter(s, hy_w1, hy_b1, hy_freq, hy_w2, hy_b2, hy_w3)
    y_b = jax.nn.silu(b_g) * (x0 * bidir_fftconv(x1 * vb, kern, hy_bias))

    qc = apply_rope(rmsnorm(c_q.reshape(b, s, 2 * C_HEADS, C_SUB), c_qn), cs_seq).reshape(b, s, C_HEADS, 2, C_SUB)
    kc = apply_rope(rmsnorm(c_k.reshape(b, s, 2 * C_HEADS, C_SUB), c_kn), cs_seq).reshape(b, s, C_HEADS, 2, C_SUB)
    vc = c_v.reshape(b, s, C_HEADS, HEAD_DIM)
    lambda_init = 0.8 - 0.6 * math.exp(-0.3 * layer_idx)
    lam = (jnp.exp(jnp.sum(lam_q1.astype(jnp.float32) * lam_k1.astype(jnp.float32)))
           - jnp.exp(jnp.sum(lam_q2.astype(jnp.float32) * lam_k2.astype(jnp.float32)))
           + lambda_init)
    oc = rmsnorm(diff_attention(qc, kc, vc, lam), c_subln) * (1.0 - lambda_init)
    y_c = jax.nn.silu(c_g) * oc.reshape(b, s, C_HEADS * HEAD_DIM)

    bg, cg, xd = jnp.split(d_p, 3, axis=-1)
    y_d = jax.nn.silu(d_g) * (bg * dwconv3(cg * xd, sc_conv_w))

    out = jnp.concatenate([y_a, y_b, y_c, y_d], axis=-1) @ w_out
    return x + gate[:, None, :] * out


def setup_inputs(seed: int = 0) -> dict:
    key = jax.random.key(seed)
    ks = jax.random.split(key, 26)

    def nrm(k, shape, scale):
        return jax.random.normal(k, shape, jnp.float32) * scale

    def gain(k, shape, noise=0.02):
        return 1.0 + noise * jax.random.normal(k, shape, jnp.float32)

    L = DEPTH
    return {
        'x': nrm(ks[0], (BATCH, SEQ, D_MODEL), 1.0),
        'c': nrm(ks[1], (BATCH, D_MODEL), 1.0),
        'norm_g': gain(ks[2], (L, D_MODEL)),
        'w_ada': nrm(ks[3], (L, D_MODEL, 3 * D_MODEL), 0.5 * D_MODEL ** -0.5),
        'b_ada': nrm(ks[4], (L, 3 * D_MODEL), 0.01),
        'w_in': nrm(ks[5], (L, D_MODEL, D_IN), D_MODEL ** -0.5),
        'w_out': nrm(ks[6], (L, D_MIX, D_MODEL), D_MIX ** -0.5),
        'a_qn': gain(ks[7], (L, HEAD_DIM)),
        'a_kn': gain(ks[8], (L, HEAD_DIM)),
        'hy_conv_w': nrm(ks[9], (L, CONV_W, 3 * GROUP_W), CONV_W ** -0.5),
        'hy_conv_b': nrm(ks[10], (L, 3 * GROUP_W), 0.01),
        'hy_w1': nrm(ks[11], (L, HY_EMB, HY_FFN), HY_EMB ** -0.5),
        'hy_b1': nrm(ks[12], (L, HY_FFN), 0.1),
        'hy_freq': gain(ks[13], (L, HY_FFN), 0.1),
        'hy_w2': nrm(ks[14], (L, HY_FFN, HY_FFN), HY_FFN ** -0.5),
        'hy_b2': nrm(ks[15], (L, HY_FFN), 0.1),
        'hy_w3': nrm(ks[16], (L, HY_FFN, 2 * GROUP_W), HY_FFN ** -0.5),
        'hy_bias': nrm(ks[17], (L, GROUP_W), 1.0),
        'c_qn': gain(ks[18], (L, C_SUB)),
        'c_kn': gain(ks[19], (L, C_SUB)),
        'lam_q1': nrm(ks[20], (L, C_SUB), 0.1),
        'lam_k1': nrm(ks[21], (L, C_SUB), 0.1),
        'lam_q2': nrm(ks[22], (L, C_SUB), 0.1),
        'lam_k2': nrm(ks[23], (L, C_SUB), 0.1),
        'c_subln': gain(ks[24], (L, HEAD_DIM)),
        'sc_conv_w': nrm(ks[25], (L, CONV_W, GROUP_W), CONV_W ** -0.5),
    }


def reference(x, c, norm_g, w_ada, b_ada, w_in, w_out, a_qn, a_kn, hy_conv_w, hy_conv_b,
              hy_w1, hy_b1, hy_freq, hy_w2, hy_b2, hy_w3, hy_bias, c_qn, c_kn,
              lam_q1, lam_k1, lam_q2, lam_k2, c_subln, sc_conv_w):
    s = x.shape[1]
    rows = s // GRID_W
    t = jnp.arange(s, dtype=jnp.int32)
    row = jnp.repeat(jnp.arange(rows, dtype=jnp.int32), GRID_W)
    col = jnp.tile(jnp.arange(GRID_W, dtype=jnp.int32), rows)
    cs_row = rope_cos_sin(row, HEAD_DIM // 2)
    cs_col = rope_cos_sin(col, HEAD_DIM // 2)
    cs_seq = rope_cos_sin(t, C_SUB)
    for l in range(DEPTH):
        x = hybrid_layer(x, c, l, cs_row, cs_col, cs_seq, norm_g[l], w_ada[l], b_ada[l], w_in[l], w_out[l],
                         a_qn[l], a_kn[l], hy_conv_w[l], hy_conv_b[l], hy_w1[l], hy_b1[l], hy_freq[l],
                         hy_w2[l], hy_b2[l], hy_w3[l], hy_bias[l], c_qn[l], c_kn[l],
                         lam_q1[l], lam_k1[l], lam_q2[l], lam_k2[l], c_subln[l], sc_conv_w[l])
    return x
```

```python
import functools
import math

import jax
import jax.numpy as jnp
from jax import lax
from jax.experimental import pallas as pl
from jax.experimental.pallas import tpu as pltpu

F32 = jnp.float32
BF16 = jnp.bfloat16

GROUP_W = 256
HEAD_DIM = 64
C_SUB = 32
HY_EMB = 33
HY_BANDS = (HY_EMB - 1) // 2
HY_SHIFT = 0.05
HY_FAST = 0.3
HY_SLOW = 1.5
HY_TARGET = 1e-2
GRID_W = 64
ROPE_THETA = 10000.0
EPS = 1e-6
LOG2E = 1.4426950408889634

LANES = 128
SUBLANES = 8
VMEM_LIMIT = 56 * 1024 * 1024
NEG_BIG = -3.0e38

HIGHEST = lax.Precision.HIGHEST


def _params(*sem):
    return pltpu.CompilerParams(dimension_semantics=sem, vmem_limit_bytes=VMEM_LIMIT)


def _dot(a, b):
    return jnp.dot(a, b, preferred_element_type=F32)


def _dot_exact(a, b):
    return jnp.dot(a, b, preferred_element_type=F32, precision=HIGHEST)


def _silu(x):
    return x * (1.0 / (1.0 + jnp.exp(-x)))


def _ada_kernel(c_ref, w_ref, b_ref, o_ref):
    o_ref[0] = _dot_exact(_silu(c_ref[...]), w_ref[0]) + b_ref[0]


def _ada(c, w_ada, b_ada):
    depth, d, d3 = w_ada.shape
    bsz = c.shape[0]
    nb = d3 // d
    return pl.pallas_call(
        _ada_kernel,
        grid=(depth, nb),
        in_specs=[
            pl.BlockSpec((bsz, d), lambda l, j: (0, 0)),
            pl.BlockSpec((1, d, d), lambda l, j: (l, 0, j)),
            pl.BlockSpec((1, 1, d), lambda l, j: (l, 0, j)),
        ],
        out_specs=pl.BlockSpec((1, bsz, d), lambda l, j: (l, 0, j)),
        out_shape=jax.ShapeDtypeStruct((depth, bsz, d3), F32),
        compiler_params=_params("arbitrary", "arbitrary"),
        name="ada",
    )(c, w_ada, b_ada.reshape(depth, 1, d3))


def _hyfilter_kernel(emb_ref, t_ref, w1_ref, b1_ref, fr_ref, w2_ref, b2_ref, w3_ref, dl_ref,
                     kk_ref, nrm_ref, *, seq, cb):
    j = pl.program_id(1)
    fr = fr_ref[0]
    h = jnp.sin(fr * (_dot_exact(w1_ref[0], emb_ref[...]) + b1_ref[0]))
    h = jnp.sin(fr * (_dot_exact(w2_ref[0], h) + b2_ref[0]))
    h = _dot_exact(w3_ref[0], h)
    window = jnp.exp(-t_ref[...] * dl_ref[...]) + HY_SHIFT
    col = j * cb + lax.broadcasted_iota(jnp.int32, (GROUP_W, cb), 1)
    kk = jnp.where(col >= seq, h[:GROUP_W], h[GROUP_W:]) * window
    kk = jnp.where(col == 0, 0.0, kk)
    kk_ref[0] = kk

    part = jnp.abs(kk[:, 0:LANES])
    for i in range(1, cb // LANES):
        part = part + jnp.abs(kk[:, i * LANES:(i + 1) * LANES])

    @pl.when(j == 0)
    def _():
        nrm_ref[0] = part

    @pl.when(j > 0)
    def _():
        nrm_ref[0] = nrm_ref[0] + part

    @pl.when(j == pl.num_programs(1) - 1)
    def _():
        tot = jnp.sum(nrm_ref[0], axis=-1, keepdims=True)
        nrm_ref[0] = jnp.broadcast_to(tot, (GROUP_W, LANES))


def _hyfilter(seq, hy_w1, hy_b1, hy_freq, hy_w2, hy_b2, hy_w3):
    depth = hy_w1.shape[0]
    ffn = hy_w1.shape[2]
    t = jnp.linspace(0.0, 1.0, seq, dtype=F32)[:, None]
    w = 2.0 * math.pi * jnp.arange(seq, dtype=F32)[:, None] / seq
    f = jnp.linspace(1e-4, HY_BANDS - 1, HY_BANDS, dtype=F32)[None, :]
    emb = jnp.concatenate([t, jnp.cos(f * w), -jnp.sin(f * w)], axis=-1)
    m = jnp.arange(2 * seq)
    pos = jnp.where(m >= seq, m - seq, jnp.where(m == 0, 0, seq - m))
    emb2 = jnp.pad(emb[pos].T, ((0, LANES - HY_EMB), (0, 0)))
    t2 = t[pos].T
    max_decay = math.log(HY_TARGET) / HY_FAST
    min_decay = math.log(HY_TARGET) / HY_SLOW
    deltas = jnp.abs(jnp.linspace(min_decay, max_decay, GROUP_W, dtype=F32))[:, None]

    w1t = jnp.pad(jnp.swapaxes(hy_w1, 1, 2), ((0, 0), (0, 0), (0, LANES - HY_EMB)))
    w2t = jnp.swapaxes(hy_w2, 1, 2)
    w3t = jnp.swapaxes(hy_w3, 1, 2)
    col = lambda v: v[:, :, None]
    cb = min(1024, 2 * seq)
    nblk = (2 * seq) // cb
    wspec = lambda r, c: pl.BlockSpec((1, r, c), lambda l, j: (l, 0, 0))
    kk, nrm = pl.pallas_call(
        functools.partial(_hyfilter_kernel, seq=seq, cb=cb),
        grid=(depth, nblk),
        in_specs=[
            pl.BlockSpec((LANES, cb), lambda l, j: (0, j)),
            pl.BlockSpec((1, cb), lambda l, j: (0, j)),
            wspec(ffn, LANES), wspec(ffn, 1), wspec(ffn, 1), wspec(ffn, ffn), wspec(ffn, 1),
            wspec(2 * GROUP_W, ffn),
            pl.BlockSpec((GROUP_W, 1), lambda l, j: (0, 0)),
        ],
        out_specs=[
            pl.BlockSpec((1, GROUP_W, cb), lambda l, j: (l, 0, j)),
            pl.BlockSpec((1, GROUP_W, LANES), lambda l, j: (l, 0, 0)),
        ],
        out_shape=[
            jax.ShapeDtypeStruct((depth, GROUP_W, 2 * seq), F32),
            jax.ShapeDtypeStruct((depth, GROUP_W, LANES), F32),
        ],
        compiler_params=_params("arbitrary", "arbitrary"),
        name="hyfilter",
    )(emb2, t2, w1t, col(hy_b1), col(hy_freq), w2t, col(hy_b2), w3t, deltas)
    return kk, nrm


def _swap16(y):
    lane = lax.broadcasted_iota(jnp.int32, y.shape, 1)
    first = (lane & 31) < 16
    return jnp.where(first, pltpu.roll(y, LANES - 16, 1), pltpu.roll(y, 16, 1))


def _norm_rope(p, gain, gmat, gsize, cos, sin):
    ms = _dot_exact(p * p, gmat) * (1.0 / gsize)
    y = p * lax.rsqrt(ms + EPS) * gain
    return y * cos + _swap16(y) * sin


def _inproj_kernel(x_ref, mod_ref, ng_ref, w_ref, aqn_ref, akn_ref, cqn_ref, ckn_ref,
                   g64_ref, g32_ref, cosa_ref, sina_ref, cosc_ref, sinc_ref,
                   qa_ref, kat_ref, va_ref, ga_ref, bp_ref, gb_ref,
                   qc_ref, kct_ref, vc_ref, gc_ref, dbg_ref, du_ref, gd_ref, *, d_model):
    x = x_ref[...]
    mod = mod_ref[0]
    shift = mod[:, 0:d_model]
    scale = mod[:, d_model:2 * d_model]
    ms = jnp.mean(x * x, axis=-1, keepdims=True)
    h = x * lax.rsqrt(ms + EPS) * ng_ref[...]
    hb = (h * (1.0 + scale) + shift).astype(BF16)

    def proj(lo, width):
        return _dot(hb, w_ref[:, lo:lo + width])

    g64 = g64_ref[...]
    g32 = g32_ref[...]
    cosa, sina = cosa_ref[...], sina_ref[...]
    cosc, sinc = cosc_ref[...], sinc_ref[...]
    gw = GROUP_W
    sa = HEAD_DIM ** -0.5 * LOG2E
    sc = C_SUB ** -0.5 * LOG2E

    off = 0
    pq = proj(off, gw)
    for i in range(gw // LANES):
        q = _norm_rope(pq[:, i * LANES:(i + 1) * LANES], aqn_ref[...], g64, HEAD_DIM, cosa, sina)
        qa_ref[:, i * LANES:(i + 1) * LANES] = (q * sa).astype(BF16)
    off += gw
    pkv = proj(off, gw)
    kt = _norm_rope(pkv[:, 0:LANES], akn_ref[...], g64, HEAD_DIM, cosa, sina).T.astype(BF16)
    hd = HEAD_DIM
    kat_ref[0, 0, 0 * hd:1 * hd, :] = kt[0:hd]
    kat_ref[0, 0, 1 * hd:2 * hd, :] = kt[0:hd]
    kat_ref[0, 0, 2 * hd:3 * hd, :] = kt[hd:2 * hd]
    kat_ref[0, 0, 3 * hd:4 * hd, :] = kt[hd:2 * hd]
    va_ref[...] = pkv[:, LANES:2 * LANES].astype(BF16)
    off += gw
    ga_ref[...] = proj(off, gw)
    off += gw
    bp_ref[...] = proj(off, 3 * gw)
    off += 3 * gw
    gb_ref[...] = proj(off, gw)
    off += gw
    pq = proj(off, gw)
    for i in range(gw // LANES):
        q = _norm_rope(pq[:, i * LANES:(i + 1) * LANES], cqn_ref[...], g32, C_SUB, cosc, sinc)
        qc_ref[:, i * LANES:(i + 1) * LANES] = (q * sc).astype(BF16)
    off += gw
    pk = proj(off, gw)
    for i in range(gw // LANES):
        k = _norm_rope(pk[:, i * LANES:(i + 1) * LANES], ckn_ref[...], g32, C_SUB, cosc, sinc)
        kct_ref[0, 0, i * LANES:(i + 1) * LANES, :] = k.T.astype(BF16)
    off += gw
    vc_ref[...] = proj(off, gw).astype(BF16)
    off += gw
    gc_ref[...] = proj(off, gw)
    off += gw
    pd = proj(off, 3 * gw)
    dbg_ref[...] = pd[:, 0:gw]
    du_ref[...] = pd[:, gw:2 * gw] * pd[:, 2 * gw:3 * gw]
    off += 3 * gw
    gd_ref[...] = proj(off, gw)


def _group_matrix(gsize):
    i = jnp.arange(LANES) // gsize
    return (i[:, None] == i[None, :]).astype(F32)


def _inproj(x2, mod3, norm_g, w_in_bf, a_qn, a_kn, c_qn, c_kn, tables, *, seq, tm):
    rows, d_model = x2.shape
    d_in = w_in_bf.shape[1]
    tps = seq // tm
    bsz = rows // seq
    gw = GROUP_W
    rep = lambda v, g: jnp.tile(v, LANES // g)[None, :]
    full = lambda r, c: pl.BlockSpec((r, c), lambda i: (0, 0))
    rowblk = lambda c: pl.BlockSpec((tm, c), lambda i: (i, 0))
    tab = pl.BlockSpec((tm, LANES), lambda i: (i % tps, 0))
    ktspec = lambda r: pl.BlockSpec((1, 1, r, tm), lambda i: (i // tps, i % tps, 0, 0))
    out_shape = [
        jax.ShapeDtypeStruct((rows, gw), BF16),
        jax.ShapeDtypeStruct((bsz, tps, 2 * LANES, tm), BF16),
        jax.ShapeDtypeStruct((rows, LANES), BF16),
        jax.ShapeDtypeStruct((rows, gw), F32),
        jax.ShapeDtypeStruct((rows, 3 * gw), F32),
        jax.ShapeDtypeStruct((rows, gw), F32),
        jax.ShapeDtypeStruct((rows, gw), BF16),
        jax.ShapeDtypeStruct((bsz, tps, gw, tm), BF16),
        jax.ShapeDtypeStruct((rows, gw), BF16),
        jax.ShapeDtypeStruct((rows, gw), F32),
        jax.ShapeDtypeStruct((rows, gw), F32),
        jax.ShapeDtypeStruct((rows, gw), F32),
        jax.ShapeDtypeStruct((rows, gw), F32),
    ]
    out_specs = [
        rowblk(gw), ktspec(2 * LANES), rowblk(LANES), rowblk(gw), rowblk(3 * gw), rowblk(gw),
        rowblk(gw), ktspec(gw), rowblk(gw), rowblk(gw), rowblk(gw), rowblk(gw), rowblk(gw),
    ]
    return pl.pallas_call(
        functools.partial(_inproj_kernel, d_model=d_model),
        grid=(rows // tm,),
        in_specs=[
            rowblk(d_model),
            pl.BlockSpec((1, 1, 3 * d_model), lambda i: (i // tps, 0, 0)),
            full(1, d_model),
            full(d_model, d_in),
            full(1, LANES), full(1, LANES), full(1, LANES), full(1, LANES),
            full(LANES, LANES), full(LANES, LANES),
            tab, tab, tab, tab,
        ],
        out_specs=out_specs,
        out_shape=out_shape,
        compiler_params=_params("arbitrary"),
        name="inproj",
    )(x2, mod3, norm_g[None, :], w_in_bf, rep(a_qn, HEAD_DIM), rep(a_kn, HEAD_DIM),
      rep(c_qn, C_SUB), rep(c_kn, C_SUB), _group_matrix(HEAD_DIM), _group_matrix(C_SUB), *tables)


def _softmax_pv(q2, kt_ref, krow, v_ref, s_ref, nc, kc):
    rows = q2.shape[0]
    ncol = v_ref.shape[1]

    def scores(c, mrun):
        s = _dot(q2, kt_ref[0, c, krow:krow + LANES, :])
        s_ref[c] = s
        for j in range(kc // LANES):
            mrun = jnp.maximum(mrun, s[:, j * LANES:(j + 1) * LANES])
        return mrun

    mrun = lax.fori_loop(0, nc, scores, jnp.full((rows, LANES), NEG_BIG, F32))
    mb = jnp.broadcast_to(jnp.max(mrun, axis=-1, keepdims=True), (rows, LANES))

    def weigh(c, carry):
        lrun, acc = carry
        s = s_ref[c]
        ps = []
        for j in range(kc // LANES):
            p = jnp.exp2(s[:, j * LANES:(j + 1) * LANES] - mb)
            lrun = lrun + p
            ps.append(p.astype(BF16))
        pb = jnp.concatenate(ps, axis=1)
        vv = v_ref[pl.ds(pl.multiple_of(c * kc, kc), kc), :]
        return lrun, acc + _dot(pb, vv)

    lrun, acc = lax.fori_loop(
        0, nc, weigh, (jnp.zeros((rows, LANES), F32), jnp.zeros((rows, ncol), F32)))
    return acc, jnp.sum(lrun, axis=-1, keepdims=True)


def _attn_a_kernel(q_ref, kt_ref, v_ref, o_ref, s_ref, *, nc, kc, tq):
    lane = lax.broadcasted_iota(jnp.int32, (tq, LANES), 1)
    low = lane < HEAD_DIM
    for h in range(2):
        qh = q_ref[:, h * LANES:(h + 1) * LANES]
        zero = jnp.zeros_like(qh)
        q2 = jnp.concatenate([jnp.where(low, qh, zero), jnp.where(low, zero, qh)], axis=0)
        acc, l = _softmax_pv(q2, kt_ref, h * LANES, v_ref, s_ref, nc, kc)
        o = acc * (1.0 / l)
        top, bot = o[0:tq], o[tq:2 * tq]
        if h == 0:
            res = jnp.where(low, top, pltpu.roll(bot, HEAD_DIM, 1))
        else:
            res = jnp.where(low, pltpu.roll(top, HEAD_DIM, 1), bot)
        o_ref[:, h * LANES:(h + 1) * LANES] = res


def _attn_a(qa, kat, va, *, seq, tq):
    rows = qa.shape[0]
    bsz, nc, _, kc = kat.shape
    qps = seq // tq
    return pl.pallas_call(
        functools.partial(_attn_a_kernel, nc=nc, kc=kc, tq=tq),
        grid=(bsz, qps),
        in_specs=[
            pl.BlockSpec((tq, GROUP_W), lambda b, i: (b * qps + i, 0)),
            pl.BlockSpec((1, nc, 2 * LANES, kc), lambda b, i: (b, 0, 0, 0)),
            pl.BlockSpec((seq, LANES), lambda b, i: (b, 0)),
        ],
        out_specs=pl.BlockSpec((tq, GROUP_W), lambda b, i: (b * qps + i, 0)),
        out_shape=jax.ShapeDtypeStruct((rows, GROUP_W), F32),
        scratch_shapes=[pltpu.VMEM((nc, 2 * tq, kc), F32)],
        compiler_params=_params("arbitrary", "arbitrary"),
        name="attn_a",
    )(qa, kat, va)


def _attn_c_kernel(q_ref, kt_ref, v_ref, lam_ref, sub_ref, g64_ref, o_ref, s_ref,
                   *, nc, kc, tq, lambda_init):
    lamv = lam_ref[...]
    lam = (jnp.exp(jnp.sum(lamv[0:1] * lamv[1:2], axis=-1, keepdims=True))
           - jnp.exp(jnp.sum(lamv[2:3] * lamv[3:4], axis=-1, keepdims=True)) + lambda_init)
    lane = lax.broadcasted_iota(jnp.int32, (tq, LANES), 1)
    lane2 = lax.broadcasted_iota(jnp.int32, (tq, GROUP_W), 1)
    out = jnp.zeros((tq, GROUP_W), F32)
    for hh in range(GROUP_W // HEAD_DIM):
        slab = hh // 2
        base = (hh % 2) * HEAD_DIM
        qh = q_ref[:, slab * LANES:(slab + 1) * LANES]
        zero = jnp.zeros_like(qh)
        in1 = (lane >= base) & (lane < base + C_SUB)
        in2 = (lane >= base + C_SUB) & (lane < base + HEAD_DIM)
        q2 = jnp.concatenate([jnp.where(in1, qh, zero), jnp.where(in2, qh, zero)], axis=0)
        acc, l = _softmax_pv(q2, kt_ref, slab * LANES, v_ref, s_ref, nc, kc)
        o = acc * (1.0 / l)
        d = o[0:tq] - lam * o[tq:2 * tq]
        mine = (lane2 >= hh * HEAD_DIM) & (lane2 < (hh + 1) * HEAD_DIM)
        out = jnp.where(mine, d, out)
    g64 = g64_ref[...]
    for i in range(GROUP_W // LANES):
        oi = out[:, i * LANES:(i + 1) * LANES]
        ms = _dot_exact(oi * oi, g64) * (1.0 / HEAD_DIM)
        o_ref[:, i * LANES:(i + 1) * LANES] = (
            oi * lax.rsqrt(ms + EPS) * sub_ref[...] * (1.0 - lambda_init))


def _attn_c(qc, kct, vc, lamv, c_subln, *, seq, tq, lambda_init):
    rows = qc.shape[0]
    bsz, nc, _, kc = kct.shape
    qps = seq // tq
    full = lambda r, c: pl.BlockSpec((r, c), lambda b, i: (0, 0))
    return pl.pallas_call(
        functools.partial(_attn_c_kernel, nc=nc, kc=kc, tq=tq, lambda_init=lambda_init),
        grid=(bsz, qps),
        in_specs=[
            pl.BlockSpec((tq, GROUP_W), lambda b, i: (b * qps + i, 0)),
            pl.BlockSpec((1, nc, GROUP_W, kc), lambda b, i: (b, 0, 0, 0)),
            pl.BlockSpec((seq, GROUP_W), lambda b, i: (b, 0)),
            full(4, C_SUB), full(1, LANES), full(LANES, LANES),
        ],
        out_specs=pl.BlockSpec((tq, GROUP_W), lambda b, i: (b * qps + i, 0)),
        out_shape=jax.ShapeDtypeStruct((rows, GROUP_W), F32),
        scratch_shapes=[pltpu.VMEM((nc, 2 * tq, kc), F32)],
        compiler_params=_params("arbitrary", "arbitrary"),
        name="attn_c",
    )(qc, kct, vc, lamv, jnp.tile(c_subln, LANES // HEAD_DIM)[None, :], _group_matrix(HEAD_DIM))


def _neighbours(u, prev_ref, next_ref, tps):
    tm = u.shape[0]
    i = pl.program_id(0)
    has_prev = (i % tps != 0).astype(F32)
    has_next = (i % tps != tps - 1).astype(F32)
    prev_row = prev_ref[SUBLANES - 1:SUBLANES, :] * has_prev
    next_row = next_ref[0:1, :] * has_next
    rid = lax.broadcasted_iota(jnp.int32, u.shape, 0)
    up = jnp.where(rid == 0, prev_row, pltpu.roll(u, 1, 0))
    dn = jnp.where(rid == tm - 1, next_row, pltpu.roll(u, tm - 1, 0))
    return up, dn


def _halo_specs(tm, width, nrows):
    per = tm // SUBLANES
    last = nrows // SUBLANES - 1
    prev = pl.BlockSpec((SUBLANES, width), lambda i: (jnp.maximum(i * per - 1, 0), 0))
    nxt = pl.BlockSpec((SUBLANES, width), lambda i: (jnp.minimum((i + 1) * per, last), 0))
    return prev, nxt


def _hypre_kernel(bp_ref, prev_ref, next_ref, w_ref, b_ref, x0_ref, z_ref, *, tps):
    u = bp_ref[...]
    up, dn = _neighbours(u, prev_ref, next_ref, tps)
    w = w_ref[...]
    y = up * w[0:1] + u * w[1:2] + dn * w[2:3] + b_ref[...]
    gw = GROUP_W
    x0_ref[...] = y[:, 0:gw]
    z_ref[...] = y[:, gw:2 * gw] * y[:, 2 * gw:3 * gw]


def _hypre(bp, conv_w, conv_b, *, seq, tm):
    rows, width = bp.shape
    prev, nxt = _halo_specs(tm, width, rows)
    rowblk = lambda c: pl.BlockSpec((tm, c), lambda i: (i, 0))
    full = lambda r, c: pl.BlockSpec((r, c), lambda i: (0, 0))
    return pl.pallas_call(
        functools.partial(_hypre_kernel, tps=seq // tm),
        grid=(rows // tm,),
        in_specs=[rowblk(width), prev, nxt, full(3, width), full(1, width)],
        out_specs=[rowblk(GROUP_W), rowblk(GROUP_W)],
        out_shape=[jax.ShapeDtypeStruct((rows, GROUP_W), F32)] * 2,
        compiler_params=_params("arbitrary"),
        name="hypre",
    )(bp, bp, bp, conv_w, conv_b[None, :])


def _hyconv_kernel(z_ref, kk_ref, nrm_ref, o_ref, acc_ref, *, nb, bsz):
    z = z_ref[0].reshape(nb * bsz, LANES)
    inv = 1.0 / nrm_ref[0]
    acc_ref[...] = jnp.zeros_like(acc_ref)
    for d in range(-(nb - 1), nb):
        win = kk_ref[0, d + nb - 1:d + nb + 1, :]
        x = jnp.concatenate([jnp.broadcast_to(win[0:1], (LANES, LANES)),
                             jnp.broadcast_to(win[1:2], (LANES, LANES))], axis=1)
        w = pltpu.roll(x, LANES, 1, stride=1, stride_axis=0)[:, 0:LANES] * inv
        m = bsz * (nb - abs(d))
        if d >= 0:
            acc_ref[bsz * d:bsz * d + m, :] += _dot_exact(z[0:m], w)
        else:
            acc_ref[0:m, :] += _dot_exact(z[-bsz * d:-bsz * d + m], w)
    o_ref[0] = acc_ref[...].reshape(nb, bsz, LANES)


def _hyconv(zt, kk3, nrm3):
    ch, nb, bsz, _ = zt.shape
    return pl.pallas_call(
        functools.partial(_hyconv_kernel, nb=nb, bsz=bsz),
        grid=(ch,),
        in_specs=[
            pl.BlockSpec((1, nb, bsz, LANES), lambda c: (c, 0, 0, 0)),
            pl.BlockSpec((1, 2 * nb, LANES), lambda c: (c, 0, 0)),
            pl.BlockSpec((1, 1, LANES), lambda c: (c, 0, 0)),
        ],
        out_specs=pl.BlockSpec((1, nb, bsz, LANES), lambda c: (c, 0, 0, 0)),
        out_shape=jax.ShapeDtypeStruct(zt.shape, F32),
        scratch_shapes=[pltpu.VMEM((nb * bsz, LANES), F32)],
        compiler_params=_params("arbitrary"),
        name="hyconv",
    )(zt, kk3, nrm3)


def _outproj_kernel(x_ref, mod_ref, oa_ref, ga_ref, x0_ref, yb_ref, z_ref, gb_ref, hb_ref,
                    oc_ref, gc_ref, dbg_ref, du_ref, dprev_ref, dnext_ref, gd_ref, scw_ref,
                    w_ref, o_ref, *, tps, d_model):
    gate = mod_ref[0][:, 2 * d_model:3 * d_model]
    y_a = _silu(ga_ref[...]) * oa_ref[...]
    z = z_ref[...]
    y_b = _silu(gb_ref[...]) * (x0_ref[...] * (yb_ref[...] + hb_ref[...] * z))
    y_c = _silu(gc_ref[...]) * oc_ref[...]
    u = du_ref[...]
    up, dn = _neighbours(u, dprev_ref, dnext_ref, tps)
    w = scw_ref[...]
    y_d = _silu(gd_ref[...]) * (dbg_ref[...] * (up * w[0:1] + u * w[1:2] + dn * w[2:3]))
    cat = jnp.concatenate([y_a, y_b, y_c, y_d], axis=1).astype(BF16)
    o_ref[...] = x_ref[...] + gate * _dot(cat, w_ref[...])


def _outproj(x2, mod3, oa, ga, x0, yb, z, gb, hy_bias, oc, gc, dbg, du, gd, sc_conv_w, w_out_bf,
             *, seq, tm):
    rows, d_model = x2.shape
    tps = seq // tm
    gw = GROUP_W
    rowblk = lambda c: pl.BlockSpec((tm, c), lambda i: (i, 0))
    full = lambda r, c: pl.BlockSpec((r, c), lambda i: (0, 0))
    prev, nxt = _halo_specs(tm, gw, rows)
    g = rowblk(gw)
    return pl.pallas_call(
        functools.partial(_outproj_kernel, tps=tps, d_model=d_model),
        grid=(rows // tm,),
        in_specs=[
            rowblk(d_model),
            pl.BlockSpec((1, 1, 3 * d_model), lambda i: (i // tps, 0, 0)),
            g, g, g, g, g, g, full(1, gw), g, g, g, g, prev, nxt, g, full(3, gw),
            full(w_out_bf.shape[0], d_model),
        ],
        out_specs=rowblk(d_model),
        out_shape=jax.ShapeDtypeStruct((rows, d_model), F32),
        compiler_params=_params("arbitrary"),
        name="outproj",
    )(x2, mod3, oa, ga, x0, yb, z, gb, hy_bias[None, :], oc, gc, dbg, du, du, du, gd,
      sc_conv_w, w_out_bf)


def _rope_tables(seq):
    def cos_sin(pos, dim):
        inv = ROPE_THETA ** (-jnp.arange(0, dim, 2, dtype=F32) / dim)
        ang = pos.astype(F32)[:, None] * inv[None, :]
        return jnp.cos(ang), jnp.sin(ang)

    t = jnp.arange(seq, dtype=jnp.int32)
    cr, sr = cos_sin(t // GRID_W, HEAD_DIM // 2)
    cc, sc = cos_sin(t % GRID_W, HEAD_DIM // 2)
    cq, sq = cos_sin(t, C_SUB)
    cosa = jnp.tile(jnp.concatenate([cr, cr, cc, cc], axis=1), (1, LANES // HEAD_DIM))
    sina = jnp.tile(jnp.concatenate([-sr, sr, -sc, sc], axis=1), (1, LANES // HEAD_DIM))
    cosc = jnp.tile(jnp.concatenate([cq, cq], axis=1), (1, LANES // C_SUB))
    sinc = jnp.tile(jnp.concatenate([-sq, sq], axis=1), (1, LANES // C_SUB))
    return cosa, sina, cosc, sinc


def _tile_rows(seq, want):
    return want if seq % want == 0 else seq


def kernel(x, c, norm_g, w_ada, b_ada, w_in, w_out, a_qn, a_kn, hy_conv_w, hy_conv_b, hy_w1, hy_b1,
           hy_freq, hy_w2, hy_b2, hy_w3, hy_bias, c_qn, c_kn, lam_q1, lam_k1, lam_q2, lam_k2,
           c_subln, sc_conv_w):
    bsz, seq, d_model = x.shape
    depth = w_in.shape[0]
    nb = seq // LANES
    tm = _tile_rows(seq, 512)
    tq = _tile_rows(seq, 256)

    tables = _rope_tables(seq)
    mod = _ada(c, w_ada, b_ada)
    kk, nrm = _hyfilter(seq, hy_w1, hy_b1, hy_freq, hy_w2, hy_b2, hy_w3)
    w_in_bf = w_in.astype(BF16)
    w_out_bf = w_out.astype(BF16)

    x2 = x.reshape(bsz * seq, d_model)
    for l in range(depth):
        mod3 = mod[l][:, None, :]
        (qa, kat, va, ga, bp, gb, qc, kct, vc, gc, dbg, du, gd) = _inproj(
            x2, mod3, norm_g[l], w_in_bf[l], a_qn[l], a_kn[l], c_qn[l], c_kn[l], tables,
            seq=seq, tm=tm)
        oa = _attn_a(qa, kat, va, seq=seq, tq=tq)
        lambda_init = 0.8 - 0.6 * math.exp(-0.3 * l)
        lamv = jnp.stack([lam_q1[l], lam_k1[l], lam_q2[l], lam_k2[l]])
        oc = _attn_c(qc, kct, vc, lamv, c_subln[l], seq=seq, tq=tq, lambda_init=lambda_init)
        x0, z = _hypre(bp, hy_conv_w[l], hy_conv_b[l], seq=seq, tm=tm)
        zt = z.reshape(bsz, nb, LANES, GROUP_W).transpose(3, 1, 0, 2)
        yt = _hyconv(zt, kk[l].reshape(GROUP_W, 2 * nb, LANES), nrm[l][:, None, :])
        yb = yt.transpose(2, 1, 3, 0).reshape(bsz * seq, GROUP_W)
        x2 = _outproj(x2, mod3, oa, ga, x0, yb, z, gb, hy_bias[l], oc, gc, dbg, du, gd,
                      sc_conv_w[l], w_out_bf[l], seq=seq, tm=tm)
    return x2.reshape(bsz, seq, d_model)
```

```python
import functools
import math

import jax
import jax.numpy as jnp
from jax import lax
from jax.experimental import pallas as pl
from jax.experimental.pallas import tpu as pltpu

F32 = jnp.float32
BF16 = jnp.bfloat16

GROUP_W = 256
HEAD_DIM = 64
C_SUB = 32
HY_EMB = 33
HY_BANDS = (HY_EMB - 1) // 2
HY_SHIFT = 0.05
HY_FAST = 0.3
HY_SLOW = 1.5
HY_TARGET = 1e-2
GRID_W = 64
ROPE_THETA = 10000.0
EPS = 1e-6
LOG2E = 1.4426950408889634

LANES = 128
SUBLANES = 8
VMEM_LIMIT = 56 * 1024 * 1024
NEG_BIG = -3.0e38

HIGHEST = lax.Precision.HIGHEST


def _params(*sem):
    return pltpu.CompilerParams(dimension_semantics=sem, vmem_limit_bytes=VMEM_LIMIT)


def _dot(a, b):
    return jnp.dot(a, b, preferred_element_type=F32)


def _dot_exact(a, b):
    return jnp.dot(a, b, preferred_element_type=F32, precision=HIGHEST)


def _silu(x):
    return x * (1.0 / (1.0 + jnp.exp(-x)))


def _ada_kernel(c_ref, w_ref, b_ref, o_ref):
    o_ref[0] = _dot_exact(_silu(c_ref[...]), w_ref[0]) + b_ref[0]


def _ada(c, w_ada, b_ada):
    depth, d, d3 = w_ada.shape
    bsz = c.shape[0]
    nb = d3 // d
    return pl.pallas_call(
        _ada_kernel,
        grid=(depth, nb),
        in_specs=[
            pl.BlockSpec((bsz, d), lambda l, j: (0, 0)),
            pl.BlockSpec((1, d, d), lambda l, j: (l, 0, j)),
            pl.BlockSpec((1, 1, d), lambda l, j: (l, 0, j)),
        ],
        out_specs=pl.BlockSpec((1, bsz, d), lambda l, j: (l, 0, j)),
        out_shape=jax.ShapeDtypeStruct((depth, bsz, d3), F32),
        compiler_params=_params("arbitrary", "arbitrary"),
        name="ada",
    )(c, w_ada, b_ada.reshape(depth, 1, d3))


def _hyfilter_kernel(emb_ref, t_ref, w1_ref, b1_ref, fr_ref, w2_ref, b2_ref, w3_ref, dl_ref,
                     kk_ref, nrm_ref, *, seq, cb):
    j = pl.program_id(1)
    fr = fr_ref[0]
    h = jnp.sin(fr * (_dot_exact(w1_ref[0], emb_ref[...]) + b1_ref[0]))
    h = jnp.sin(fr * (_dot_exact(w2_ref[0], h) + b2_ref[0]))
    h = _dot_exact(w3_ref[0], h)
    window = jnp.exp(-t_ref[...] * dl_ref[...]) + HY_SHIFT
    col = j * cb + lax.broadcasted_iota(jnp.int32, (GROUP_W, cb), 1)
    kk = jnp.where(col >= seq, h[:GROUP_W], h[GROUP_W:]) * window
    kk = jnp.where(col == 0, 0.0, kk)
    kk_ref[0] = kk

    part = jnp.abs(kk[:, 0:LANES])
    for i in range(1, cb // LANES):
        part = part + jnp.abs(kk[:, i * LANES:(i + 1) * LANES])

    @pl.when(j == 0)
    def _():
        nrm_ref[0] = part

    @pl.when(j > 0)
    def _():
        nrm_ref[0] = nrm_ref[0] + part

    @pl.when(j == pl.num_programs(1) - 1)
    def _():
        tot = jnp.sum(nrm_ref[0], axis=-1, keepdims=True)
        nrm_ref[0] = jnp.broadcast_to(tot, (GROUP_W, LANES))


def _hyfilter(seq, hy_w1, hy_b1, hy_freq, hy_w2, hy_b2, hy_w3):
    depth = hy_w1.shape[0]
    ffn = hy_w1.shape[2]
    t = jnp.linspace(0.0, 1.0, seq, dtype=F32)[:, None]
    w = 2.0 * math.pi * jnp.arange(seq, dtype=F32)[:, None] / seq
    f = jnp.linspace(1e-4, HY_BANDS - 1, HY_BANDS, dtype=F32)[None, :]
    emb = jnp.concatenate([t, jnp.cos(f * w), -jnp.sin(f * w)], axis=-1)
    m = jnp.arange(2 * seq)
    pos = jnp.where(m >= seq, m - seq, jnp.where(m == 0, 0, seq - m))
    emb2 = jnp.pad(emb[pos].T, ((0, LANES - HY_EMB), (0, 0)))
    t2 = t[pos].T
    max_decay = math.log(HY_TARGET) / HY_FAST
    min_decay = math.log(HY_TARGET) / HY_SLOW
    deltas = jnp.abs(jnp.linspace(min_decay, max_decay, GROUP_W, dtype=F32))[:, None]

    w1t = jnp.pad(jnp.swapaxes(hy_w1, 1, 2), ((0, 0), (0, 0), (0, LANES - HY_EMB)))
    w2t = jnp.swapaxes(hy_w2, 1, 2)
    w3t = jnp.swapaxes(hy_w3, 1, 2)
    col = lambda v: v[:, :, None]
    cb = min(1024, 2 * seq)
    nblk = (2 * seq) // cb
    wspec = lambda r, c: pl.BlockSpec((1, r, c), lambda l, j: (l, 0, 0))
    kk, nrm = pl.pallas_call(
        functools.partial(_hyfilter_kernel, seq=seq, cb=cb),
        grid=(depth, nblk),
        in_specs=[
            pl.BlockSpec((LANES, cb), lambda l, j: (0, j)),
            pl.BlockSpec((1, cb), lambda l, j: (0, j)),
            wspec(ffn, LANES), wspec(ffn, 1), wspec(ffn, 1), wspec(ffn, ffn), wspec(ffn, 1),
            wspec(2 * GROUP_W, ffn),
            pl.BlockSpec((GROUP_W, 1), lambda l, j: (0, 0)),
        ],
        out_specs=[
            pl.BlockSpec((1, GROUP_W, cb), lambda l, j: (l, 0, j)),
            pl.BlockSpec((1, GROUP_W, LANES), lambda l, j: (l, 0, 0)),
        ],
        out_shape=[
            jax.ShapeDtypeStruct((depth, GROUP_W, 2 * seq), F32),
            jax.ShapeDtypeStruct((depth, GROUP_W, LANES), F32),
        ],
        compiler_params=_params("arbitrary", "arbitrary"),
        name="hyfilter",
    )(emb2, t2, w1t, col(hy_b1), col(hy_freq), w2t, col(hy_b2), w3t, deltas)
    return kk, nrm


def _swap16(y):
    lane = lax.broadcasted_iota(jnp.int32, y.shape, 1)
    first = (lane & 31) < 16
    return jnp.where(first, pltpu.roll(y, LANES - 16, 1), pltpu.roll(y, 16, 1))


def _norm_rope(p, gain, gmat, gsize, cos, sin):
    ms = _dot_exact(p * p, gmat) * (1.0 / gsize)
    y = p * lax.rsqrt(ms + EPS) * gain
    return y * cos + _swap16(y) * sin


def _store_values_with_ones(v_ref, slab, v):
    low = lax.broadcasted_iota(jnp.int32, v.shape, 1) < HEAD_DIM
    v_ref[:, slab * LANES:(slab + 1) * LANES] = jnp.where(low, v, 1.0).astype(BF16)
    v_ref[:, (slab + 1) * LANES:(slab + 2) * LANES] = jnp.where(
        low, pltpu.roll(v, HEAD_DIM, 1), 1.0).astype(BF16)


def _inproj_kernel(x_ref, mod_ref, ng_ref, w_ref, aqn_ref, akn_ref, cqn_ref, ckn_ref,
                   g64_ref, g32_ref, cosa_ref, sina_ref, cosc_ref, sinc_ref,
                   qa_ref, kat_ref, va_ref, ga_ref, bp_ref, gb_ref,
                   qc_ref, kct_ref, vc_ref, gc_ref, dbg_ref, du_ref, gd_ref, *, d_model):
    x = x_ref[...]
    mod = mod_ref[0]
    shift = mod[:, 0:d_model]
    scale = mod[:, d_model:2 * d_model]
    ms = jnp.mean(x * x, axis=-1, keepdims=True)
    h = x * lax.rsqrt(ms + EPS) * ng_ref[...]
    hb = (h * (1.0 + scale) + shift).astype(BF16)

    def proj(lo, width):
        return _dot(hb, w_ref[:, lo:lo + width])

    g64 = g64_ref[...]
    g32 = g32_ref[...]
    cosa, sina = cosa_ref[...], sina_ref[...]
    cosc, sinc = cosc_ref[...], sinc_ref[...]
    gw = GROUP_W
    sa = HEAD_DIM ** -0.5 * LOG2E
    sc = C_SUB ** -0.5 * LOG2E

    off = 0
    pq = proj(off, gw)
    for i in range(gw // LANES):
        q = _norm_rope(pq[:, i * LANES:(i + 1) * LANES], aqn_ref[...], g64, HEAD_DIM, cosa, sina)
        qa_ref[:, i * LANES:(i + 1) * LANES] = (q * sa).astype(BF16)
    off += gw
    pkv = proj(off, gw)
    kt = _norm_rope(pkv[:, 0:LANES], akn_ref[...], g64, HEAD_DIM, cosa, sina).T.astype(BF16)
    hd = HEAD_DIM
    kat_ref[0, 0, 0 * hd:1 * hd, :] = kt[0:hd]
    kat_ref[0, 0, 1 * hd:2 * hd, :] = kt[0:hd]
    kat_ref[0, 0, 2 * hd:3 * hd, :] = kt[hd:2 * hd]
    kat_ref[0, 0, 3 * hd:4 * hd, :] = kt[hd:2 * hd]
    _store_values_with_ones(va_ref, 0, pkv[:, LANES:2 * LANES])
    off += gw
    ga_ref[...] = proj(off, gw)
    off += gw
    bp_ref[...] = proj(off, 3 * gw)
    off += 3 * gw
    gb_ref[...] = proj(off, gw)
    off += gw
    pq = proj(off, gw)
    for i in range(gw // LANES):
        q = _norm_rope(pq[:, i * LANES:(i + 1) * LANES], cqn_ref[...], g32, C_SUB, cosc, sinc)
        qc_ref[:, i * LANES:(i + 1) * LANES] = (q * sc).astype(BF16)
    off += gw
    pk = proj(off, gw)
    for i in range(gw // LANES):
        k = _norm_rope(pk[:, i * LANES:(i + 1) * LANES], ckn_ref[...], g32, C_SUB, cosc, sinc)
        kct_ref[0, 0, i * LANES:(i + 1) * LANES, :] = k.T.astype(BF16)
    off += gw
    pv = proj(off, gw)
    for i in range(gw // LANES):
        _store_values_with_ones(vc_ref, 2 * i, pv[:, i * LANES:(i + 1) * LANES])
    off += gw
    gc_ref[...] = proj(off, gw)
    off += gw
    pd = proj(off, 3 * gw)
    dbg_ref[...] = pd[:, 0:gw]
    du_ref[...] = pd[:, gw:2 * gw] * pd[:, 2 * gw:3 * gw]
    off += 3 * gw
    gd_ref[...] = proj(off, gw)


def _group_matrix(gsize):
    i = jnp.arange(LANES) // gsize
    return (i[:, None] == i[None, :]).astype(F32)


def _inproj(x2, mod3, norm_g, w_in_bf, a_qn, a_kn, c_qn, c_kn, tables, *, seq, tm):
    rows, d_model = x2.shape
    d_in = w_in_bf.shape[1]
    tps = seq // tm
    bsz = rows // seq
    gw = GROUP_W
    rep = lambda v, g: jnp.tile(v, LANES // g)[None, :]
    full = lambda r, c: pl.BlockSpec((r, c), lambda i: (0, 0))
    rowblk = lambda c: pl.BlockSpec((tm, c), lambda i: (i, 0))
    tab = pl.BlockSpec((tm, LANES), lambda i: (i % tps, 0))
    ktspec = lambda r: pl.BlockSpec((1, 1, r, tm), lambda i: (i // tps, i % tps, 0, 0))
    out_shape = [
        jax.ShapeDtypeStruct((rows, gw), BF16),
        jax.ShapeDtypeStruct((bsz, tps, 2 * LANES, tm), BF16),
        jax.ShapeDtypeStruct((rows, 2 * LANES), BF16),
        jax.ShapeDtypeStruct((rows, gw), F32),
        jax.ShapeDtypeStruct((rows, 3 * gw), F32),
        jax.ShapeDtypeStruct((rows, gw), F32),
        jax.ShapeDtypeStruct((rows, gw), BF16),
        jax.ShapeDtypeStruct((bsz, tps, gw, tm), BF16),
        jax.ShapeDtypeStruct((rows, 2 * gw), BF16),
        jax.ShapeDtypeStruct((rows, gw), F32),
        jax.ShapeDtypeStruct((rows, gw), F32),
        jax.ShapeDtypeStruct((rows, gw), F32),
        jax.ShapeDtypeStruct((rows, gw), F32),
    ]
    out_specs = [
        rowblk(gw), ktspec(2 * LANES), rowblk(2 * LANES), rowblk(gw), rowblk(3 * gw), rowblk(gw),
        rowblk(gw), ktspec(gw), rowblk(2 * gw), rowblk(gw), rowblk(gw), rowblk(gw), rowblk(gw),
    ]
    return pl.pallas_call(
        functools.partial(_inproj_kernel, d_model=d_model),
        grid=(rows // tm,),
        in_specs=[
            rowblk(d_model),
            pl.BlockSpec((1, 1, 3 * d_model), lambda i: (i // tps, 0, 0)),
            full(1, d_model),
            full(d_model, d_in),
            full(1, LANES), full(1, LANES), full(1, LANES), full(1, LANES),
            full(LANES, LANES), full(LANES, LANES),
            tab, tab, tab, tab,
        ],
        out_specs=out_specs,
        out_shape=out_shape,
        compiler_params=_params("arbitrary"),
        name="inproj",
    )(x2, mod3, norm_g[None, :], w_in_bf, rep(a_qn, HEAD_DIM), rep(a_kn, HEAD_DIM),
      rep(c_qn, C_SUB), rep(c_kn, C_SUB), _group_matrix(HEAD_DIM), _group_matrix(C_SUB), *tables)


def _softmax_pv(q2, kt_ref, krow, v_ref, vcol, s_ref, nc, kc):
    rows = q2.shape[0]
    mrun = jnp.full((rows, LANES), NEG_BIG, F32)
    for c in range(nc):
        s = _dot(q2, kt_ref[0, c, krow:krow + LANES, :])
        s_ref[c] = s
        for j in range(kc // LANES):
            mrun = jnp.maximum(mrun, s[:, j * LANES:(j + 1) * LANES])
    mb = jnp.broadcast_to(jnp.max(mrun, axis=-1, keepdims=True), (rows, LANES))
    acc = jnp.zeros((rows, LANES), F32)
    for c in range(nc):
        s = s_ref[c]
        pb = jnp.concatenate(
            [jnp.exp2(s[:, j * LANES:(j + 1) * LANES] - mb).astype(BF16) for j in range(kc // LANES)],
            axis=1)
        acc = acc + _dot(pb, v_ref[c * kc:(c + 1) * kc, vcol:vcol + LANES])
    return acc


def _normalise(acc):
    return acc * (1.0 / pltpu.roll(acc, HEAD_DIM, 1))


def _attn_a_kernel(q_ref, kt_ref, v_ref, o_ref, s_ref, *, nc, kc, tq):
    lane = lax.broadcasted_iota(jnp.int32, (tq, LANES), 1)
    low = lane < HEAD_DIM
    for h in range(2):
        qh = q_ref[:, h * LANES:(h + 1) * LANES]
        zero = jnp.zeros_like(qh)
        q2 = jnp.concatenate([jnp.where(low, qh, zero), jnp.where(low, zero, qh)], axis=0)
        o = _normalise(_softmax_pv(q2, kt_ref, h * LANES, v_ref, h * LANES, s_ref.at[h], nc, kc))
        o_ref[:, h * LANES:(h + 1) * LANES] = jnp.where(
            low, o[0:tq], pltpu.roll(o[tq:2 * tq], HEAD_DIM, 1))


def _attn_a(qa, kat, va, *, seq, tq):
    rows = qa.shape[0]
    bsz, nc, _, kc = kat.shape
    qps = seq // tq
    return pl.pallas_call(
        functools.partial(_attn_a_kernel, nc=nc, kc=kc, tq=tq),
        grid=(bsz, qps),
        in_specs=[
            pl.BlockSpec((tq, GROUP_W), lambda b, i: (b * qps + i, 0)),
            pl.BlockSpec((1, nc, 2 * LANES, kc), lambda b, i: (b, 0, 0, 0)),
            pl.BlockSpec((seq, 2 * LANES), lambda b, i: (b, 0)),
        ],
        out_specs=pl.BlockSpec((tq, GROUP_W), lambda b, i: (b * qps + i, 0)),
        out_shape=jax.ShapeDtypeStruct((rows, GROUP_W), F32),
        scratch_shapes=[pltpu.VMEM((2, nc, 2 * tq, kc), F32)],
        compiler_params=_params("arbitrary", "arbitrary"),
        name="attn_a",
    )(qa, kat, va)


def _attn_c_kernel(q_ref, kt_ref, v_ref, lam_ref, sub_ref, g64_ref, o_ref, s_ref,
                   *, nc, kc, tq, lambda_init):
    lamv = lam_ref[...]
    lam = (jnp.exp(jnp.sum(lamv[0:1] * lamv[1:2], axis=-1, keepdims=True))
           - jnp.exp(jnp.sum(lamv[2:3] * lamv[3:4], axis=-1, keepdims=True)) + lambda_init)
    lane = lax.broadcasted_iota(jnp.int32, (tq, LANES), 1)
    low = lane < HEAD_DIM
    g64 = g64_ref[...]
    for slab in range(GROUP_W // LANES):
        qh = q_ref[:, slab * LANES:(slab + 1) * LANES]
        zero = jnp.zeros_like(qh)
        ds = []
        for half in range(2):
            hh = 2 * slab + half
            base = half * HEAD_DIM
            in1 = (lane >= base) & (lane < base + C_SUB)
            in2 = (lane >= base + C_SUB) & (lane < base + HEAD_DIM)
            q2 = jnp.concatenate([jnp.where(in1, qh, zero), jnp.where(in2, qh, zero)], axis=0)
            o = _normalise(
                _softmax_pv(q2, kt_ref, slab * LANES, v_ref, hh * LANES, s_ref.at[half], nc, kc))
            ds.append(o[0:tq] - lam * o[tq:2 * tq])
        oi = jnp.where(low, ds[0], pltpu.roll(ds[1], HEAD_DIM, 1))
        ms = _dot_exact(oi * oi, g64) * (1.0 / HEAD_DIM)
        o_ref[:, slab * LANES:(slab + 1) * LANES] = (
            oi * lax.rsqrt(ms + EPS) * sub_ref[...] * (1.0 - lambda_init))


def _attn_c(qc, kct, vc, lamv, c_subln, *, seq, tq, lambda_init):
    rows = qc.shape[0]
    bsz, nc, _, kc = kct.shape
    qps = seq // tq
    full = lambda r, c: pl.BlockSpec((r, c), lambda b, i: (0, 0))
    return pl.pallas_call(
        functools.partial(_attn_c_kernel, nc=nc, kc=kc, tq=tq, lambda_init=lambda_init),
        grid=(bsz, qps),
        in_specs=[
            pl.BlockSpec((tq, GROUP_W), lambda b, i: (b * qps + i, 0)),
            pl.BlockSpec((1, nc, GROUP_W, kc), lambda b, i: (b, 0, 0, 0)),
            pl.BlockSpec((seq, 2 * GROUP_W), lambda b, i: (b, 0)),
            full(4, C_SUB), full(1, LANES), full(LANES, LANES),
        ],
        out_specs=pl.BlockSpec((tq, GROUP_W), lambda b, i: (b * qps + i, 0)),
        out_shape=jax.ShapeDtypeStruct((rows, GROUP_W), F32),
        scratch_shapes=[pltpu.VMEM((2, nc, 2 * tq, kc), F32)],
        compiler_params=_params("arbitrary", "arbitrary"),
        name="attn_c",
    )(qc, kct, vc, lamv, jnp.tile(c_subln, LANES // HEAD_DIM)[None, :], _group_matrix(HEAD_DIM))


def _neighbours(u, prev_ref, next_ref, tps):
    tm = u.shape[0]
    i = pl.program_id(0)
    has_prev = (i % tps != 0).astype(F32)
    has_next = (i % tps != tps - 1).astype(F32)
    prev_row = prev_ref[SUBLANES - 1:SUBLANES, :] * has_prev
    next_row = next_ref[0:1, :] * has_next
    rid = lax.broadcasted_iota(jnp.int32, u.shape, 0)
    up = jnp.where(rid == 0, prev_row, pltpu.roll(u, 1, 0))
    dn = jnp.where(rid == tm - 1, next_row, pltpu.roll(u, tm - 1, 0))
    return up, dn


def _halo_specs(tm, width, nrows):
    per = tm // SUBLANES
    last = nrows // SUBLANES - 1
    prev = pl.BlockSpec((SUBLANES, width), lambda i: (jnp.maximum(i * per - 1, 0), 0))
    nxt = pl.BlockSpec((SUBLANES, width), lambda i: (jnp.minimum((i + 1) * per, last), 0))
    return prev, nxt


def _hypre_kernel(bp_ref, prev_ref, next_ref, w_ref, b_ref, x0_ref, z_ref, *, tps):
    u = bp_ref[...]
    up, dn = _neighbours(u, prev_ref, next_ref, tps)
    w = w_ref[...]
    y = up * w[0:1] + u * w[1:2] + dn * w[2:3] + b_ref[...]
    gw = GROUP_W
    x0_ref[...] = y[:, 0:gw]
    z_ref[...] = y[:, gw:2 * gw] * y[:, 2 * gw:3 * gw]


def _hypre(bp, conv_w, conv_b, *, seq, tm):
    rows, width = bp.shape
    prev, nxt = _halo_specs(tm, width, rows)
    rowblk = lambda c: pl.BlockSpec((tm, c), lambda i: (i, 0))
    full = lambda r, c: pl.BlockSpec((r, c), lambda i: (0, 0))
    return pl.pallas_call(
        functools.partial(_hypre_kernel, tps=seq // tm),
        grid=(rows // tm,),
        in_specs=[rowblk(width), prev, nxt, full(3, width), full(1, width)],
        out_specs=[rowblk(GROUP_W), rowblk(GROUP_W)],
        out_shape=[jax.ShapeDtypeStruct((rows, GROUP_W), F32)] * 2,
        compiler_params=_params("arbitrary"),
        name="hypre",
    )(bp, bp, bp, conv_w, conv_b[None, :])


def _hyconv_kernel(z_ref, kk_ref, nrm_ref, o_ref, acc_ref, *, nb, bsz):
    z = z_ref[0].reshape(nb * bsz, LANES)
    kk = kk_ref[0] * (1.0 / nrm_ref[0])
    upper = (lax.broadcasted_iota(jnp.int32, (LANES, LANES), 1)
             >= lax.broadcasted_iota(jnp.int32, (LANES, LANES), 0))
    rolled = {}

    def circulant(m):
        if m not in rolled:
            rolled[m] = pltpu.roll(jnp.broadcast_to(kk[m:m + 1], (LANES, LANES)), 0, 1,
                                   stride=1, stride_axis=0)
        return rolled[m]

    def toeplitz(d):
        return jnp.where(upper, circulant(d + nb), circulant(d + nb - 1)).astype(BF16)

    acc_ref[...] = jnp.zeros_like(acc_ref)
    for d in range(0, nb, 2):
        m = bsz * (nb - d)
        out = _dot(z[0:m].astype(BF16), jnp.concatenate([toeplitz(d), toeplitz(d + 1)], axis=1))
        acc_ref[bsz * d:bsz * d + m, :] += out[:, 0:LANES]
        acc_ref[bsz * (d + 1):bsz * d + m, :] += out[0:m - bsz, LANES:2 * LANES]
    for e in range(1, nb - 1, 2):
        m = bsz * (nb - e)
        out = _dot(z[bsz * e:bsz * e + m].astype(BF16),
                   jnp.concatenate([toeplitz(-e), toeplitz(-e - 1)], axis=1))
        acc_ref[0:m, :] += out[:, 0:LANES]
        acc_ref[0:m - bsz, :] += out[bsz:m, LANES:2 * LANES]
    e = nb - 1
    acc_ref[0:bsz, :] += _dot(z[bsz * e:bsz * nb].astype(BF16), toeplitz(-e))
    o_ref[0] = acc_ref[...].reshape(nb, bsz, LANES)


def _hyconv(zt, kk3, nrm3):
    ch, nb, bsz, _ = zt.shape
    assert nb % 2 == 0
    return pl.pallas_call(
        functools.partial(_hyconv_kernel, nb=nb, bsz=bsz),
        grid=(ch,),
        in_specs=[
            pl.BlockSpec((1, nb, bsz, LANES), lambda c: (c, 0, 0, 0)),
            pl.BlockSpec((1, 2 * nb, LANES), lambda c: (c, 0, 0)),
            pl.BlockSpec((1, 1, LANES), lambda c: (c, 0, 0)),
        ],
        out_specs=pl.BlockSpec((1, nb, bsz, LANES), lambda c: (c, 0, 0, 0)),
        out_shape=jax.ShapeDtypeStruct(zt.shape, F32),
        scratch_shapes=[pltpu.VMEM((nb * bsz, LANES), F32)],
        compiler_params=_params("arbitrary"),
        name="hyconv",
    )(zt, kk3, nrm3)


def _outproj_kernel(x_ref, mod_ref, oa_ref, ga_ref, x0_ref, yb_ref, z_ref, gb_ref, hb_ref,
                    oc_ref, gc_ref, dbg_ref, du_ref, dprev_ref, dnext_ref, gd_ref, scw_ref,
                    w_ref, o_ref, *, tps, d_model):
    gate = mod_ref[0][:, 2 * d_model:3 * d_model]
    y_a = _silu(ga_ref[...]) * oa_ref[...]
    z = z_ref[...]
    y_b = _silu(gb_ref[...]) * (x0_ref[...] * (yb_ref[...] + hb_ref[...] * z))
    y_c = _silu(gc_ref[...]) * oc_ref[...]
    u = du_ref[...]
    up, dn = _neighbours(u, dprev_ref, dnext_ref, tps)
    w = scw_ref[...]
    y_d = _silu(gd_ref[...]) * (dbg_ref[...] * (up * w[0:1] + u * w[1:2] + dn * w[2:3]))
    cat = jnp.concatenate([y_a, y_b, y_c, y_d], axis=1).astype(BF16)
    o_ref[...] = x_ref[...] + gate * _dot(cat, w_ref[...])


def _outproj(x2, mod3, oa, ga, x0, yb, z, gb, hy_bias, oc, gc, dbg, du, gd, sc_conv_w, w_out_bf,
             *, seq, tm):
    rows, d_model = x2.shape
    tps = seq // tm
    gw = GROUP_W
    rowblk = lambda c: pl.BlockSpec((tm, c), lambda i: (i, 0))
    full = lambda r, c: pl.BlockSpec((r, c), lambda i: (0, 0))
    prev, nxt = _halo_specs(tm, gw, rows)
    g = rowblk(gw)
    return pl.pallas_call(
        functools.partial(_outproj_kernel, tps=tps, d_model=d_model),
        grid=(rows // tm,),
        in_specs=[
            rowblk(d_model),
            pl.BlockSpec((1, 1, 3 * d_model), lambda i: (i // tps, 0, 0)),
            g, g, g, g, g, g, full(1, gw), g, g, g, g, prev, nxt, g, full(3, gw),
            full(w_out_bf.shape[0], d_model),
        ],
        out_specs=rowblk(d_model),
        out_shape=jax.ShapeDtypeStruct((rows, d_model), F32),
        compiler_params=_params("arbitrary"),
        name="outproj",
    )(x2, mod3, oa, ga, x0, yb, z, gb, hy_bias[None, :], oc, gc, dbg, du, du, du, gd,
      sc_conv_w, w_out_bf)


def _rope_tables(seq):
    def cos_sin(pos, dim):
        inv = ROPE_THETA ** (-jnp.arange(0, dim, 2, dtype=F32) / dim)
        ang = pos.astype(F32)[:, None] * inv[None, :]
        return jnp.cos(ang), jnp.sin(ang)

    t = jnp.arange(seq, dtype=jnp.int32)
    cr, sr = cos_sin(t // GRID_W, HEAD_DIM // 2)
    cc, sc = cos_sin(t % GRID_W, HEAD_DIM // 2)
    cq, sq = cos_sin(t, C_SUB)
    cosa = jnp.tile(jnp.concatenate([cr, cr, cc, cc], axis=1), (1, LANES // HEAD_DIM))
    sina = jnp.tile(jnp.concatenate([-sr, sr, -sc, sc], axis=1), (1, LANES // HEAD_DIM))
    cosc = jnp.tile(jnp.concatenate([cq, cq], axis=1), (1, LANES // C_SUB))
    sinc = jnp.tile(jnp.concatenate([-sq, sq], axis=1), (1, LANES // C_SUB))
    return cosa, sina, cosc, sinc


def _tile_rows(seq, want):
    return want if seq % want == 0 else seq


def kernel(x, c, norm_g, w_ada, b_ada, w_in, w_out, a_qn, a_kn, hy_conv_w, hy_conv_b, hy_w1, hy_b1,
           hy_freq, hy_w2, hy_b2, hy_w3, hy_bias, c_qn, c_kn, lam_q1, lam_k1, lam_q2, lam_k2,
           c_subln, sc_conv_w):
    bsz, seq, d_model = x.shape
    depth = w_in.shape[0]
    nb = seq // LANES
    tm = _tile_rows(seq, 512)
    tq = _tile_rows(seq, 256)

    tables = _rope_tables(seq)
    mod = _ada(c, w_ada, b_ada)
    kk, nrm = _hyfilter(seq, hy_w1, hy_b1, hy_freq, hy_w2, hy_b2, hy_w3)
    w_in_bf = w_in.astype(BF16)
    w_out_bf = w_out.astype(BF16)

    x2 = x.reshape(bsz * seq, d_model)
    for l in range(depth):
        mod3 = mod[l][:, None, :]
        (qa, kat, va, ga, bp, gb, qc, kct, vc, gc, dbg, du, gd) = _inproj(
            x2, mod3, norm_g[l], w_in_bf[l], a_qn[l], a_kn[l], c_qn[l], c_kn[l], tables,
            seq=seq, tm=tm)
        oa = _attn_a(qa, kat, va, seq=seq, tq=tq)
        lambda_init = 0.8 - 0.6 * math.exp(-0.3 * l)
        lamv = jnp.stack([lam_q1[l], lam_k1[l], lam_q2[l], lam_k2[l]])
        oc = _attn_c(qc, kct, vc, lamv, c_subln[l], seq=seq, tq=tq, lambda_init=lambda_init)
        x0, z = _hypre(bp, hy_conv_w[l], hy_conv_b[l], seq=seq, tm=tm)
        zt = z.reshape(bsz, nb, LANES, GROUP_W).transpose(3, 1, 0, 2)
        yt = _hyconv(zt, kk[l].reshape(GROUP_W, 2 * nb, LANES), nrm[l][:, None, :])
        yb = yt.transpose(2, 1, 3, 0).reshape(bsz * seq, GROUP_W)
        x2 = _outproj(x2, mod3, oa, ga, x0, yb, z, gb, hy_bias[l], oc, gc, dbg, du, gd,
                      sc_conv_w[l], w_out_bf[l], seq=seq, tm=tm)
    return x2.reshape(bsz, seq, d_model)
```

```python
import functools
import math

import jax
import jax.numpy as jnp
from jax import lax
from jax.experimental import pallas as pl
from jax.experimental.pallas import tpu as pltpu

F32 = jnp.float32
BF16 = jnp.bfloat16

GROUP_W = 256
HEAD_DIM = 64
C_SUB = 32
HY_EMB = 33
HY_BANDS = (HY_EMB - 1) // 2
HY_SHIFT = 0.05
HY_FAST = 0.3
HY_SLOW = 1.5
HY_TARGET = 1e-2
GRID_W = 64
ROPE_THETA = 10000.0
EPS = 1e-6
LOG2E = 1.4426950408889634

LANES = 128
SUBLANES = 8
VMEM_LIMIT = 56 * 1024 * 1024
NEG_BIG = -3.0e38

HIGHEST = lax.Precision.HIGHEST


def _params(*sem):
    return pltpu.CompilerParams(dimension_semantics=sem, vmem_limit_bytes=VMEM_LIMIT)


def _dot(a, b):
    return jnp.dot(a, b, preferred_element_type=F32)


def _dot_exact(a, b):
    return jnp.dot(a, b, preferred_element_type=F32, precision=HIGHEST)


def _silu(x):
    return x * (1.0 / (1.0 + jnp.exp(-x)))


def _ada_kernel(c_ref, w_ref, b_ref, o_ref):
    o_ref[0] = _dot_exact(_silu(c_ref[...]), w_ref[0]) + b_ref[0]


def _ada(c, w_ada, b_ada):
    depth, d, d3 = w_ada.shape
    bsz = c.shape[0]
    nb = d3 // d
    return pl.pallas_call(
        _ada_kernel,
        grid=(depth, nb),
        in_specs=[
            pl.BlockSpec((bsz, d), lambda l, j: (0, 0)),
            pl.BlockSpec((1, d, d), lambda l, j: (l, 0, j)),
            pl.BlockSpec((1, 1, d), lambda l, j: (l, 0, j)),
        ],
        out_specs=pl.BlockSpec((1, bsz, d), lambda l, j: (l, 0, j)),
        out_shape=jax.ShapeDtypeStruct((depth, bsz, d3), F32),
        compiler_params=_params("arbitrary", "arbitrary"),
        name="ada",
    )(c, w_ada, b_ada.reshape(depth, 1, d3))


def _hyfilter_kernel(emb_ref, t_ref, w1_ref, b1_ref, fr_ref, w2_ref, b2_ref, w3_ref, dl_ref,
                     kk_ref, nrm_ref, *, seq, cb):
    j = pl.program_id(1)
    fr = fr_ref[0]
    h = jnp.sin(fr * (_dot_exact(w1_ref[0], emb_ref[...]) + b1_ref[0]))
    h = jnp.sin(fr * (_dot_exact(w2_ref[0], h) + b2_ref[0]))
    h = _dot_exact(w3_ref[0], h)
    window = jnp.exp(-t_ref[...] * dl_ref[...]) + HY_SHIFT
    col = j * cb + lax.broadcasted_iota(jnp.int32, (GROUP_W, cb), 1)
    kk = jnp.where(col >= seq, h[:GROUP_W], h[GROUP_W:]) * window
    kk = jnp.where(col == 0, 0.0, kk)
    kk_ref[0] = kk

    part = jnp.abs(kk[:, 0:LANES])
    for i in range(1, cb // LANES):
        part = part + jnp.abs(kk[:, i * LANES:(i + 1) * LANES])

    @pl.when(j == 0)
    def _():
        nrm_ref[0] = part

    @pl.when(j > 0)
    def _():
        nrm_ref[0] = nrm_ref[0] + part

    @pl.when(j == pl.num_programs(1) - 1)
    def _():
        tot = jnp.sum(nrm_ref[0], axis=-1, keepdims=True)
        nrm_ref[0] = jnp.broadcast_to(tot, (GROUP_W, LANES))


def _hyfilter(seq, hy_w1, hy_b1, hy_freq, hy_w2, hy_b2, hy_w3):
    depth = hy_w1.shape[0]
    ffn = hy_w1.shape[2]
    t = jnp.linspace(0.0, 1.0, seq, dtype=F32)[:, None]
    w = 2.0 * math.pi * jnp.arange(seq, dtype=F32)[:, None] / seq
    f = jnp.linspace(1e-4, HY_BANDS - 1, HY_BANDS, dtype=F32)[None, :]
    emb = jnp.concatenate([t, jnp.cos(f * w), -jnp.sin(f * w)], axis=-1)
    both = lambda v: jnp.concatenate([v[0:1], v[:0:-1], v], axis=0)
    emb2 = jnp.pad(both(emb).T, ((0, LANES - HY_EMB), (0, 0)))
    t2 = both(t).T
    max_decay = math.log(HY_TARGET) / HY_FAST
    min_decay = math.log(HY_TARGET) / HY_SLOW
    deltas = jnp.abs(jnp.linspace(min_decay, max_decay, GROUP_W, dtype=F32))[:, None]

    w1t = jnp.pad(jnp.swapaxes(hy_w1, 1, 2), ((0, 0), (0, 0), (0, LANES - HY_EMB)))
    w2t = jnp.swapaxes(hy_w2, 1, 2)
    w3t = jnp.swapaxes(hy_w3, 1, 2)
    col = lambda v: v[:, :, None]
    cb = min(1024, 2 * seq)
    nblk = (2 * seq) // cb
    wspec = lambda r, c: pl.BlockSpec((1, r, c), lambda l, j: (l, 0, 0))
    kk, nrm = pl.pallas_call(
        functools.partial(_hyfilter_kernel, seq=seq, cb=cb),
        grid=(depth, nblk),
        in_specs=[
            pl.BlockSpec((LANES, cb), lambda l, j: (0, j)),
            pl.BlockSpec((1, cb), lambda l, j: (0, j)),
            wspec(ffn, LANES), wspec(ffn, 1), wspec(ffn, 1), wspec(ffn, ffn), wspec(ffn, 1),
            wspec(2 * GROUP_W, ffn),
            pl.BlockSpec((GROUP_W, 1), lambda l, j: (0, 0)),
        ],
        out_specs=[
            pl.BlockSpec((1, GROUP_W, cb), lambda l, j: (l, 0, j)),
            pl.BlockSpec((1, GROUP_W, LANES), lambda l, j: (l, 0, 0)),
        ],
        out_shape=[
            jax.ShapeDtypeStruct((depth, GROUP_W, 2 * seq), F32),
            jax.ShapeDtypeStruct((depth, GROUP_W, LANES), F32),
        ],
        compiler_params=_params("arbitrary", "arbitrary"),
        name="hyfilter",
    )(emb2, t2, w1t, col(hy_b1), col(hy_freq), w2t, col(hy_b2), w3t, deltas)
    return kk, nrm


def _swap16(y):
    lane = lax.broadcasted_iota(jnp.int32, y.shape, 1)
    first = (lane & 31) < 16
    return jnp.where(first, pltpu.roll(y, LANES - 16, 1), pltpu.roll(y, 16, 1))


def _norm_rope(p, gain, gmat, gsize, cos, sin):
    ms = _dot_exact(p * p, gmat) * (1.0 / gsize)
    y = p * lax.rsqrt(ms + EPS) * gain
    return y * cos + _swap16(y) * sin


def _store_values_with_ones(v_ref, slab, v):
    low = lax.broadcasted_iota(jnp.int32, v.shape, 1) < HEAD_DIM
    v_ref[:, slab * LANES:(slab + 1) * LANES] = jnp.where(low, v, 1.0).astype(BF16)
    v_ref[:, (slab + 1) * LANES:(slab + 2) * LANES] = jnp.where(
        low, pltpu.roll(v, HEAD_DIM, 1), 1.0).astype(BF16)


def _inproj_kernel(x_ref, mod_ref, ng_ref, w_ref, aqn_ref, akn_ref, cqn_ref, ckn_ref,
                   g64_ref, g32_ref, cosa_ref, sina_ref, cosc_ref, sinc_ref,
                   qa_ref, kat_ref, va_ref, ga_ref, bp_ref, gb_ref,
                   qc_ref, kct_ref, vc_ref, gc_ref, dbg_ref, du_ref, gd_ref, *, d_model):
    x = x_ref[...]
    mod = mod_ref[0]
    shift = mod[:, 0:d_model]
    scale = mod[:, d_model:2 * d_model]
    ms = jnp.mean(x * x, axis=-1, keepdims=True)
    h = x * lax.rsqrt(ms + EPS) * ng_ref[...]
    hb = (h * (1.0 + scale) + shift).astype(BF16)

    def proj(lo, width):
        return _dot(hb, w_ref[:, lo:lo + width])

    g64 = g64_ref[...]
    g32 = g32_ref[...]
    cosa, sina = cosa_ref[...], sina_ref[...]
    cosc, sinc = cosc_ref[...], sinc_ref[...]
    gw = GROUP_W
    sa = HEAD_DIM ** -0.5 * LOG2E
    sc = C_SUB ** -0.5 * LOG2E

    off = 0
    pq = proj(off, gw)
    for i in range(gw // LANES):
        q = _norm_rope(pq[:, i * LANES:(i + 1) * LANES], aqn_ref[...], g64, HEAD_DIM, cosa, sina)
        qa_ref[:, i * LANES:(i + 1) * LANES] = (q * sa).astype(BF16)
    off += gw
    pkv = proj(off, gw)
    kt = _norm_rope(pkv[:, 0:LANES], akn_ref[...], g64, HEAD_DIM, cosa, sina).T.astype(BF16)
    hd = HEAD_DIM
    kat_ref[0, 0, 0 * hd:1 * hd, :] = kt[0:hd]
    kat_ref[0, 0, 1 * hd:2 * hd, :] = kt[0:hd]
    kat_ref[0, 0, 2 * hd:3 * hd, :] = kt[hd:2 * hd]
    kat_ref[0, 0, 3 * hd:4 * hd, :] = kt[hd:2 * hd]
    _store_values_with_ones(va_ref, 0, pkv[:, LANES:2 * LANES])
    off += gw
    ga_ref[...] = proj(off, gw)
    off += gw
    bp_ref[...] = proj(off, 3 * gw)
    off += 3 * gw
    gb_ref[...] = proj(off, gw)
    off += gw
    pq = proj(off, gw)
    for i in range(gw // LANES):
        q = _norm_rope(pq[:, i * LANES:(i + 1) * LANES], cqn_ref[...], g32, C_SUB, cosc, sinc)
        qc_ref[:, i * LANES:(i + 1) * LANES] = (q * sc).astype(BF16)
    off += gw
    pk = proj(off, gw)
    for i in range(gw // LANES):
        k = _norm_rope(pk[:, i * LANES:(i + 1) * LANES], ckn_ref[...], g32, C_SUB, cosc, sinc)
        kct_ref[0, 0, i * LANES:(i + 1) * LANES, :] = k.T.astype(BF16)
    off += gw
    pv = proj(off, gw)
    for i in range(gw // LANES):
        _store_values_with_ones(vc_ref, 2 * i, pv[:, i * LANES:(i + 1) * LANES])
    off += gw
    gc_ref[...] = proj(off, gw)
    off += gw
    pd = proj(off, 3 * gw)
    dbg_ref[...] = pd[:, 0:gw]
    du_ref[...] = pd[:, gw:2 * gw] * pd[:, 2 * gw:3 * gw]
    off += 3 * gw
    gd_ref[...] = proj(off, gw)


def _group_matrix(gsize):
    i = jnp.arange(LANES) // gsize
    return (i[:, None] == i[None, :]).astype(F32)


def _inproj(x2, mod3, norm_g, w_in_bf, a_qn, a_kn, c_qn, c_kn, tables, *, seq, tm):
    rows, d_model = x2.shape
    d_in = w_in_bf.shape[1]
    tps = seq // tm
    bsz = rows // seq
    gw = GROUP_W
    rep = lambda v, g: jnp.tile(v, LANES // g)[None, :]
    full = lambda r, c: pl.BlockSpec((r, c), lambda i: (0, 0))
    rowblk = lambda c: pl.BlockSpec((tm, c), lambda i: (i, 0))
    tab = pl.BlockSpec((tm, LANES), lambda i: (i % tps, 0))
    ktspec = lambda r: pl.BlockSpec((1, 1, r, tm), lambda i: (i // tps, i % tps, 0, 0))
    out_shape = [
        jax.ShapeDtypeStruct((rows, gw), BF16),
        jax.ShapeDtypeStruct((bsz, tps, 2 * LANES, tm), BF16),
        jax.ShapeDtypeStruct((rows, 2 * LANES), BF16),
        jax.ShapeDtypeStruct((rows, gw), F32),
        jax.ShapeDtypeStruct((rows, 3 * gw), F32),
        jax.ShapeDtypeStruct((rows, gw), F32),
        jax.ShapeDtypeStruct((rows, gw), BF16),
        jax.ShapeDtypeStruct((bsz, tps, gw, tm), BF16),
        jax.ShapeDtypeStruct((rows, 2 * gw), BF16),
        jax.ShapeDtypeStruct((rows, gw), F32),
        jax.ShapeDtypeStruct((rows, gw), F32),
        jax.ShapeDtypeStruct((rows, gw), F32),
        jax.ShapeDtypeStruct((rows, gw), F32),
    ]
    out_specs = [
        rowblk(gw), ktspec(2 * LANES), rowblk(2 * LANES), rowblk(gw), rowblk(3 * gw), rowblk(gw),
        rowblk(gw), ktspec(gw), rowblk(2 * gw), rowblk(gw), rowblk(gw), rowblk(gw), rowblk(gw),
    ]
    return pl.pallas_call(
        functools.partial(_inproj_kernel, d_model=d_model),
        grid=(rows // tm,),
        in_specs=[
            rowblk(d_model),
            pl.BlockSpec((1, 1, 3 * d_model), lambda i: (i // tps, 0, 0)),
            full(1, d_model),
            full(d_model, d_in),
            full(1, LANES), full(1, LANES), full(1, LANES), full(1, LANES),
            full(LANES, LANES), full(LANES, LANES),
            tab, tab, tab, tab,
        ],
        out_specs=out_specs,
        out_shape=out_shape,
        compiler_params=_params("arbitrary"),
        name="inproj",
    )(x2, mod3, norm_g[None, :], w_in_bf, rep(a_qn, HEAD_DIM), rep(a_kn, HEAD_DIM),
      rep(c_qn, C_SUB), rep(c_kn, C_SUB), _group_matrix(HEAD_DIM), _group_matrix(C_SUB), *tables)


def _stage(nc, kc, rows, qk=None, pv=None):
    mrun = jnp.full((rows, LANES), NEG_BIG, F32)
    acc = jnp.zeros((rows, LANES), F32)
    for c in range(nc):
        if qk is not None:
            q2, kt_ref, krow, s_out = qk
            s = _dot(q2, kt_ref[0, c, krow:krow + LANES, :])
            s_out[c] = s
            for j in range(kc // LANES):
                mrun = jnp.maximum(mrun, s[:, j * LANES:(j + 1) * LANES])
        if pv is not None:
            s_in, mb, v_ref, vcol = pv
            s = s_in[c]
            pb = jnp.concatenate(
                [jnp.exp2(s[:, j * LANES:(j + 1) * LANES] - mb).astype(BF16)
                 for j in range(kc // LANES)], axis=1)
            acc = acc + _dot(pb, v_ref[c * kc:(c + 1) * kc, vcol:vcol + LANES])
    mb_new = jnp.broadcast_to(jnp.max(mrun, axis=-1, keepdims=True), (rows, LANES))
    return mb_new, acc


def _normalise(acc):
    return acc * (1.0 / pltpu.roll(acc, HEAD_DIM, 1))


def _init_pipeline(*refs):
    @pl.when(pl.program_id(0) == 0)
    def _():
        for r in refs:
            r[...] = jnp.zeros(r.shape, r.dtype)


def _attn_a_kernel(q_ref, kt_ref, v_ref, vprev_ref, o0_ref, o1_ref, s_ref, mb_ref, *, nc, kc, tq):
    rows = 2 * tq
    low = lax.broadcasted_iota(jnp.int32, (tq, LANES), 1) < HEAD_DIM
    _init_pipeline(s_ref.at[1], mb_ref)

    def stack(h):
        qh = q_ref[:, h * LANES:(h + 1) * LANES]
        zero = jnp.zeros_like(qh)
        return jnp.concatenate([jnp.where(low, qh, zero), jnp.where(low, zero, qh)], axis=0)

    def finish(acc):
        o = _normalise(acc)
        return jnp.where(low, o[0:tq], pltpu.roll(o[tq:2 * tq], HEAD_DIM, 1))

    mb0, acc1 = _stage(nc, kc, rows, qk=(stack(0), kt_ref, 0, s_ref.at[0]),
                       pv=(s_ref.at[1], mb_ref[...], vprev_ref, 0))
    o1_ref[...] = finish(acc1)
    mb1, acc0 = _stage(nc, kc, rows, qk=(stack(1), kt_ref, LANES, s_ref.at[1]),
                       pv=(s_ref.at[0], mb0, v_ref, 0))
    mb_ref[...] = mb1
    o0_ref[...] = finish(acc0)


def _pipeline_specs(tq, seq, n_tiles, q_width, kt_rows, nc, kc, v_width, last_unit):
    nq = seq // tq
    cur = lambda i: jnp.minimum(i, n_tiles - 1)
    prev = lambda i: jnp.maximum(i - 1, 0)
    in_specs = [
        pl.BlockSpec((tq, q_width), lambda i: (cur(i), 0)),
        pl.BlockSpec((1, nc, kt_rows, kc), lambda i: (cur(i) // nq, 0, 0, 0)),
        pl.BlockSpec((seq, v_width), lambda i: (cur(i) // nq, 0)),
        pl.BlockSpec((seq, LANES), lambda i: (prev(i) // nq, last_unit)),
    ]
    out_specs = [
        pl.BlockSpec((tq, LANES), lambda i: (cur(i), 0)),
        pl.BlockSpec((tq, LANES), lambda i: (prev(i), 0)),
    ]
    return in_specs, out_specs


def _attn_a(qa, kat, va, *, seq, tq):
    rows = qa.shape[0]
    _, nc, _, kc = kat.shape
    n_tiles = rows // tq
    in_specs, out_specs = _pipeline_specs(tq, seq, n_tiles, GROUP_W, 2 * LANES, nc, kc, 2 * LANES, 1)
    return pl.pallas_call(
        functools.partial(_attn_a_kernel, nc=nc, kc=kc, tq=tq),
        grid=(n_tiles + 1,),
        in_specs=in_specs,
        out_specs=out_specs,
        out_shape=[jax.ShapeDtypeStruct((rows, LANES), F32)] * 2,
        scratch_shapes=[pltpu.VMEM((2, nc, 2 * tq, kc), F32), pltpu.VMEM((2 * tq, LANES), F32)],
        compiler_params=_params("arbitrary"),
        name="attn_a",
    )(qa, kat, va, va)


def _attn_c_kernel(q_ref, kt_ref, v_ref, vprev_ref, lam_ref, sub_ref, g64_ref, o0_ref, o1_ref,
                   s_ref, mb_ref, d2_ref, *, nc, kc, tq, lambda_init):
    rows = 2 * tq
    lamv = lam_ref[...]
    lam = (jnp.exp(jnp.sum(lamv[0:1] * lamv[1:2], axis=-1, keepdims=True))
           - jnp.exp(jnp.sum(lamv[2:3] * lamv[3:4], axis=-1, keepdims=True)) + lambda_init)
    lane = lax.broadcasted_iota(jnp.int32, (tq, LANES), 1)
    low = lane < HEAD_DIM
    _init_pipeline(s_ref.at[1], mb_ref, d2_ref)

    def stack(u):
        qh = q_ref[:, (u // 2) * LANES:(u // 2 + 1) * LANES]
        zero = jnp.zeros_like(qh)
        base = (u % 2) * HEAD_DIM
        in1 = (lane >= base) & (lane < base + C_SUB)
        in2 = (lane >= base + C_SUB) & (lane < base + HEAD_DIM)
        return jnp.concatenate([jnp.where(in1, qh, zero), jnp.where(in2, qh, zero)], axis=0)

    def qk(u):
        return (stack(u), kt_ref, (u // 2) * LANES, s_ref.at[u % 2])

    def pv(u, mb):
        return (s_ref.at[u % 2], mb, v_ref, u * LANES)

    def differential(acc):
        o = _normalise(acc)
        return o[0:tq] - lam * o[tq:2 * tq]

    def sub_norm(d_even, d_odd):
        oi = jnp.where(low, d_even, pltpu.roll(d_odd, HEAD_DIM, 1))
        ms = _dot_exact(oi * oi, g64_ref[...]) * (1.0 / HEAD_DIM)
        return oi * lax.rsqrt(ms + EPS) * sub_ref[...] * (1.0 - lambda_init)

    mb0, acc3 = _stage(nc, kc, rows, qk=qk(0), pv=(s_ref.at[1], mb_ref[...], vprev_ref, 0))
    o1_ref[...] = sub_norm(d2_ref[...], differential(acc3))
    mb1, acc0 = _stage(nc, kc, rows, qk=qk(1), pv=pv(0, mb0))
    mb2, acc1 = _stage(nc, kc, rows, qk=qk(2), pv=pv(1, mb1))
    o0_ref[...] = sub_norm(differential(acc0), differential(acc1))
    mb3, acc2 = _stage(nc, kc, rows, qk=qk(3), pv=pv(2, mb2))
    d2_ref[...] = differential(acc2)
    mb_ref[...] = mb3


def _attn_c(qc, kct, vc, lamv, c_subln, *, seq, tq, lambda_init):
    rows = qc.shape[0]
    _, nc, _, kc = kct.shape
    n_tiles = rows // tq
    full = lambda r, c: pl.BlockSpec((r, c), lambda i: (0, 0))
    in_specs, out_specs = _pipeline_specs(tq, seq, n_tiles, GROUP_W, GROUP_W, nc, kc, 2 * GROUP_W, 3)
    return pl.pallas_call(
        functools.partial(_attn_c_kernel, nc=nc, kc=kc, tq=tq, lambda_init=lambda_init),
        grid=(n_tiles + 1,),
        in_specs=in_specs + [full(4, C_SUB), full(1, LANES), full(LANES, LANES)],
        out_specs=out_specs,
        out_shape=[jax.ShapeDtypeStruct((rows, LANES), F32)] * 2,
        scratch_shapes=[pltpu.VMEM((2, nc, 2 * tq, kc), F32), pltpu.VMEM((2 * tq, LANES), F32),
                        pltpu.VMEM((tq, LANES), F32)],
        compiler_params=_params("arbitrary"),
        name="attn_c",
    )(qc, kct, vc, vc, lamv, jnp.tile(c_subln, LANES // HEAD_DIM)[None, :], _group_matrix(HEAD_DIM))


def _neighbours(u, prev_ref, next_ref, tps):
    tm = u.shape[0]
    i = pl.program_id(0)
    has_prev = (i % tps != 0).astype(F32)
    has_next = (i % tps != tps - 1).astype(F32)
    prev_row = prev_ref[SUBLANES - 1:SUBLANES, :] * has_prev
    next_row = next_ref[0:1, :] * has_next
    rid = lax.broadcasted_iota(jnp.int32, u.shape, 0)
    up = jnp.where(rid == 0, prev_row, pltpu.roll(u, 1, 0))
    dn = jnp.where(rid == tm - 1, next_row, pltpu.roll(u, tm - 1, 0))
    return up, dn


def _halo_specs(tm, width, nrows):
    per = tm // SUBLANES
    last = nrows // SUBLANES - 1
    prev = pl.BlockSpec((SUBLANES, width), lambda i: (jnp.maximum(i * per - 1, 0), 0))
    nxt = pl.BlockSpec((SUBLANES, width), lambda i: (jnp.minimum((i + 1) * per, last), 0))
    return prev, nxt


def _hypre_kernel(bp_ref, prev_ref, next_ref, w_ref, b_ref, x0_ref, z_ref, *, tps):
    u = bp_ref[...]
    up, dn = _neighbours(u, prev_ref, next_ref, tps)
    w = w_ref[...]
    y = up * w[0:1] + u * w[1:2] + dn * w[2:3] + b_ref[...]
    gw = GROUP_W
    x0_ref[...] = y[:, 0:gw]
    z_ref[...] = y[:, gw:2 * gw] * y[:, 2 * gw:3 * gw]


def _hypre(bp, conv_w, conv_b, *, seq, tm):
    rows, width = bp.shape
    prev, nxt = _halo_specs(tm, width, rows)
    rowblk = lambda c: pl.BlockSpec((tm, c), lambda i: (i, 0))
    full = lambda r, c: pl.BlockSpec((r, c), lambda i: (0, 0))
    return pl.pallas_call(
        functools.partial(_hypre_kernel, tps=seq // tm),
        grid=(rows // tm,),
        in_specs=[rowblk(width), prev, nxt, full(3, width), full(1, width)],
        out_specs=[rowblk(GROUP_W), rowblk(GROUP_W)],
        out_shape=[jax.ShapeDtypeStruct((rows, GROUP_W), F32)] * 2,
        compiler_params=_params("arbitrary"),
        name="hypre",
    )(bp, bp, bp, conv_w, conv_b[None, :])


def _hyconv_kernel(z_ref, kk_ref, nrm_ref, o_ref, acc_ref, *, nb, bsz):
    z = z_ref[0].reshape(nb * bsz, LANES)
    kk = kk_ref[0] * (1.0 / nrm_ref[0])
    upper = (lax.broadcasted_iota(jnp.int32, (LANES, LANES), 1)
             >= lax.broadcasted_iota(jnp.int32, (LANES, LANES), 0))
    rolled = {}

    def circulant(m):
        if m not in rolled:
            rolled[m] = pltpu.roll(jnp.broadcast_to(kk[m:m + 1], (LANES, LANES)), 0, 1,
                                   stride=1, stride_axis=0)
        return rolled[m]

    def toeplitz(d):
        return jnp.where(upper, circulant(d + nb), circulant(d + nb - 1)).astype(BF16)

    acc_ref[...] = jnp.zeros_like(acc_ref)
    for d in range(0, nb, 2):
        m = bsz * (nb - d)
        out = _dot(z[0:m].astype(BF16), jnp.concatenate([toeplitz(d), toeplitz(d + 1)], axis=1))
        acc_ref[bsz * d:bsz * d + m, :] += out[:, 0:LANES]
        acc_ref[bsz * (d + 1):bsz * d + m, :] += out[0:m - bsz, LANES:2 * LANES]
    for e in range(1, nb - 1, 2):
        m = bsz * (nb - e)
        out = _dot(z[bsz * e:bsz * e + m].astype(BF16),
                   jnp.concatenate([toeplitz(-e), toeplitz(-e - 1)], axis=1))
        acc_ref[0:m, :] += out[:, 0:LANES]
        acc_ref[0:m - bsz, :] += out[bsz:m, LANES:2 * LANES]
    e = nb - 1
    acc_ref[0:bsz, :] += _dot(z[bsz * e:bsz * nb].astype(BF16), toeplitz(-e))
    o_ref[0] = acc_ref[...].reshape(nb, bsz, LANES)


def _hyconv(zt, kk3, nrm3):
    ch, nb, bsz, _ = zt.shape
    assert nb % 2 == 0
    return pl.pallas_call(
        functools.partial(_hyconv_kernel, nb=nb, bsz=bsz),
        grid=(ch,),
        in_specs=[
            pl.BlockSpec((1, nb, bsz, LANES), lambda c: (c, 0, 0, 0)),
            pl.BlockSpec((1, 2 * nb, LANES), lambda c: (c, 0, 0)),
            pl.BlockSpec((1, 1, LANES), lambda c: (c, 0, 0)),
        ],
        out_specs=pl.BlockSpec((1, nb, bsz, LANES), lambda c: (c, 0, 0, 0)),
        out_shape=jax.ShapeDtypeStruct(zt.shape, F32),
        scratch_shapes=[pltpu.VMEM((nb * bsz, LANES), F32)],
        compiler_params=_params("arbitrary"),
        name="hyconv",
    )(zt, kk3, nrm3)


def _outproj_kernel(x_ref, mod_ref, oa0_ref, oa1_ref, ga_ref, x0_ref, yb_ref, z_ref, gb_ref, hb_ref,
                    oc0_ref, oc1_ref, gc_ref, dbg_ref, du_ref, dprev_ref, dnext_ref, gd_ref, scw_ref,
                    w_ref, o_ref, *, tps, d_model):
    gate = mod_ref[0][:, 2 * d_model:3 * d_model]
    y_a = _silu(ga_ref[...]) * jnp.concatenate([oa0_ref[...], oa1_ref[...]], axis=1)
    z = z_ref[...]
    y_b = _silu(gb_ref[...]) * (x0_ref[...] * (yb_ref[...] + hb_ref[...] * z))
    y_c = _silu(gc_ref[...]) * jnp.concatenate([oc0_ref[...], oc1_ref[...]], axis=1)
    u = du_ref[...]
    up, dn = _neighbours(u, dprev_ref, dnext_ref, tps)
    w = scw_ref[...]
    y_d = _silu(gd_ref[...]) * (dbg_ref[...] * (up * w[0:1] + u * w[1:2] + dn * w[2:3]))
    cat = jnp.concatenate([y_a, y_b, y_c, y_d], axis=1).astype(BF16)
    o_ref[...] = x_ref[...] + gate * _dot(cat, w_ref[...])


def _outproj(x2, mod3, oa, ga, x0, yb, z, gb, hy_bias, oc, gc, dbg, du, gd, sc_conv_w, w_out_bf,
             *, seq, tm):
    rows, d_model = x2.shape
    tps = seq // tm
    gw = GROUP_W
    rowblk = lambda c: pl.BlockSpec((tm, c), lambda i: (i, 0))
    full = lambda r, c: pl.BlockSpec((r, c), lambda i: (0, 0))
    prev, nxt = _halo_specs(tm, gw, rows)
    g = rowblk(gw)
    hf = rowblk(LANES)
    return pl.pallas_call(
        functools.partial(_outproj_kernel, tps=tps, d_model=d_model),
        grid=(rows // tm,),
        in_specs=[
            rowblk(d_model),
            pl.BlockSpec((1, 1, 3 * d_model), lambda i: (i // tps, 0, 0)),
            hf, hf, g, g, g, g, g, full(1, gw), hf, hf, g, g, g, prev, nxt, g, full(3, gw),
            full(w_out_bf.shape[0], d_model),
        ],
        out_specs=rowblk(d_model),
        out_shape=jax.ShapeDtypeStruct((rows, d_model), F32),
        compiler_params=_params("arbitrary"),
        name="outproj",
    )(x2, mod3, *oa, ga, x0, yb, z, gb, hy_bias[None, :], *oc, gc, dbg, du, du, du, gd,
      sc_conv_w, w_out_bf)


def _rope_tables(seq):
    def cos_sin(pos, dim):
        inv = ROPE_THETA ** (-jnp.arange(0, dim, 2, dtype=F32) / dim)
        ang = pos.astype(F32)[:, None] * inv[None, :]
        return jnp.cos(ang), jnp.sin(ang)

    t = jnp.arange(seq, dtype=jnp.int32)
    cr, sr = cos_sin(t // GRID_W, HEAD_DIM // 2)
    cc, sc = cos_sin(t % GRID_W, HEAD_DIM // 2)
    cq, sq = cos_sin(t, C_SUB)
    cosa = jnp.tile(jnp.concatenate([cr, cr, cc, cc], axis=1), (1, LANES // HEAD_DIM))
    sina = jnp.tile(jnp.concatenate([-sr, sr, -sc, sc], axis=1), (1, LANES // HEAD_DIM))
    cosc = jnp.tile(jnp.concatenate([cq, cq], axis=1), (1, LANES // C_SUB))
    sinc = jnp.tile(jnp.concatenate([-sq, sq], axis=1), (1, LANES // C_SUB))
    return cosa, sina, cosc, sinc


def _tile_rows(seq, want):
    return want if seq % want == 0 else seq


def kernel(x, c, norm_g, w_ada, b_ada, w_in, w_out, a_qn, a_kn, hy_conv_w, hy_conv_b, hy_w1, hy_b1,
           hy_freq, hy_w2, hy_b2, hy_w3, hy_bias, c_qn, c_kn, lam_q1, lam_k1, lam_q2, lam_k2,
           c_subln, sc_conv_w):
    bsz, seq, d_model = x.shape
    depth = w_in.shape[0]
    nb = seq // LANES
    tm = _tile_rows(seq, 512)
    tq = _tile_rows(seq, 256)

    tables = _rope_tables(seq)
    mod = _ada(c, w_ada, b_ada)
    kk, nrm = _hyfilter(seq, hy_w1, hy_b1, hy_freq, hy_w2, hy_b2, hy_w3)
    w_in_bf = w_in.astype(BF16)
    w_out_bf = w_out.astype(BF16)

    x2 = x.reshape(bsz * seq, d_model)
    for l in range(depth):
        mod3 = mod[l][:, None, :]
        (qa, kat, va, ga, bp, gb, qc, kct, vc, gc, dbg, du, gd) = _inproj(
            x2, mod3, norm_g[l], w_in_bf[l], a_qn[l], a_kn[l], c_qn[l], c_kn[l], tables,
            seq=seq, tm=tm)
        oa = _attn_a(qa, kat, va, seq=seq, tq=tq)
        lambda_init = 0.8 - 0.6 * math.exp(-0.3 * l)
        lamv = jnp.stack([lam_q1[l], lam_k1[l], lam_q2[l], lam_k2[l]])
        oc = _attn_c(qc, kct, vc, lamv, c_subln[l], seq=seq, tq=tq, lambda_init=lambda_init)
        x0, z = _hypre(bp, hy_conv_w[l], hy_conv_b[l], seq=seq, tm=tm)
        zt = z.reshape(bsz, nb, LANES, GROUP_W).transpose(3, 1, 0, 2)
        yt = _hyconv(zt, kk[l].reshape(GROUP_W, 2 * nb, LANES), nrm[l][:, None, :])
        yb = yt.transpose(2, 1, 3, 0).reshape(bsz * seq, GROUP_W)
        x2 = _outproj(x2, mod3, oa, ga, x0, yb, z, gb, hy_bias[l], oc, gc, dbg, du, gd,
                      sc_conv_w[l], w_out_bf[l], seq=seq, tm=tm)
    return x2.reshape(bsz, seq, d_model)
```

```python
import functools
import math

import jax
import jax.numpy as jnp
from jax import lax
from jax.experimental import pallas as pl
from jax.experimental.pallas import tpu as pltpu

F32 = jnp.float32
BF16 = jnp.bfloat16

GROUP_W = 256
HEAD_DIM = 64
C_SUB = 32
HY_EMB = 33
HY_BANDS = (HY_EMB - 1) // 2
HY_SHIFT = 0.05
HY_FAST = 0.3
HY_SLOW = 1.5
HY_TARGET = 1e-2
GRID_W = 64
ROPE_THETA = 10000.0
EPS = 1e-6
LOG2E = 1.4426950408889634

LANES = 128
SUBLANES = 8
VMEM_LIMIT = 56 * 1024 * 1024
NEG_BIG = -3.0e38

HIGHEST = lax.Precision.HIGHEST


def _params(*sem):
    return pltpu.CompilerParams(dimension_semantics=sem, vmem_limit_bytes=VMEM_LIMIT)


def _dot(a, b):
    return jnp.dot(a, b, preferred_element_type=F32)


def _dot_exact(a, b):
    return jnp.dot(a, b, preferred_element_type=F32, precision=HIGHEST)


def _silu(x):
    return x * (1.0 / (1.0 + jnp.exp(-x)))


def _ada_kernel(c_ref, w_ref, b_ref, o_ref):
    o_ref[0] = _dot_exact(_silu(c_ref[...]), w_ref[0]) + b_ref[0]


def _ada(c, w_ada, b_ada):
    depth, d, d3 = w_ada.shape
    bsz = c.shape[0]
    nb = d3 // d
    return pl.pallas_call(
        _ada_kernel,
        grid=(depth, nb),
        in_specs=[
            pl.BlockSpec((bsz, d), lambda l, j: (0, 0)),
            pl.BlockSpec((1, d, d), lambda l, j: (l, 0, j)),
            pl.BlockSpec((1, 1, d), lambda l, j: (l, 0, j)),
        ],
        out_specs=pl.BlockSpec((1, bsz, d), lambda l, j: (l, 0, j)),
        out_shape=jax.ShapeDtypeStruct((depth, bsz, d3), F32),
        compiler_params=_params("arbitrary", "arbitrary"),
        name="ada",
    )(c, w_ada, b_ada.reshape(depth, 1, d3))


def _hyfilter_kernel(emb_ref, t_ref, w1_ref, b1_ref, fr_ref, w2_ref, b2_ref, w3_ref, dl_ref,
                     kk_ref, nrm_ref, *, seq, cb):
    j = pl.program_id(1)
    fr = fr_ref[0]
    h = jnp.sin(fr * (_dot_exact(w1_ref[0], emb_ref[...]) + b1_ref[0]))
    h = jnp.sin(fr * (_dot_exact(w2_ref[0], h) + b2_ref[0]))
    h = _dot_exact(w3_ref[0], h)
    window = jnp.exp(-t_ref[...] * dl_ref[...]) + HY_SHIFT
    col = j * cb + lax.broadcasted_iota(jnp.int32, (GROUP_W, cb), 1)
    kk = jnp.where(col >= seq, h[:GROUP_W], h[GROUP_W:]) * window
    kk = jnp.where(col == 0, 0.0, kk)
    kk_ref[0] = kk

    part = jnp.abs(kk[:, 0:LANES])
    for i in range(1, cb // LANES):
        part = part + jnp.abs(kk[:, i * LANES:(i + 1) * LANES])

    @pl.when(j == 0)
    def _():
        nrm_ref[0] = part

    @pl.when(j > 0)
    def _():
        nrm_ref[0] = nrm_ref[0] + part

    @pl.when(j == pl.num_programs(1) - 1)
    def _():
        tot = jnp.sum(nrm_ref[0], axis=-1, keepdims=True)
        nrm_ref[0] = jnp.broadcast_to(tot, (GROUP_W, LANES))


def _hyfilter(seq, hy_w1, hy_b1, hy_freq, hy_w2, hy_b2, hy_w3):
    depth = hy_w1.shape[0]
    ffn = hy_w1.shape[2]
    t = jnp.linspace(0.0, 1.0, seq, dtype=F32)[:, None]
    w = 2.0 * math.pi * jnp.arange(seq, dtype=F32)[:, None] / seq
    f = jnp.linspace(1e-4, HY_BANDS - 1, HY_BANDS, dtype=F32)[None, :]
    emb = jnp.concatenate([t, jnp.cos(f * w), -jnp.sin(f * w)], axis=-1)
    both = lambda v: jnp.concatenate([v[0:1], v[:0:-1], v], axis=0)
    emb2 = jnp.pad(both(emb).T, ((0, LANES - HY_EMB), (0, 0)))
    t2 = both(t).T
    max_decay = math.log(HY_TARGET) / HY_FAST
    min_decay = math.log(HY_TARGET) / HY_SLOW
    deltas = jnp.abs(jnp.linspace(min_decay, max_decay, GROUP_W, dtype=F32))[:, None]

    w1t = jnp.pad(jnp.swapaxes(hy_w1, 1, 2), ((0, 0), (0, 0), (0, LANES - HY_EMB)))
    w2t = jnp.swapaxes(hy_w2, 1, 2)
    w3t = jnp.swapaxes(hy_w3, 1, 2)
    col = lambda v: v[:, :, None]
    cb = min(1024, 2 * seq)
    nblk = (2 * seq) // cb
    wspec = lambda r, c: pl.BlockSpec((1, r, c), lambda l, j: (l, 0, 0))
    kk, nrm = pl.pallas_call(
        functools.partial(_hyfilter_kernel, seq=seq, cb=cb),
        grid=(depth, nblk),
        in_specs=[
            pl.BlockSpec((LANES, cb), lambda l, j: (0, j)),
            pl.BlockSpec((1, cb), lambda l, j: (0, j)),
            wspec(ffn, LANES), wspec(ffn, 1), wspec(ffn, 1), wspec(ffn, ffn), wspec(ffn, 1),
            wspec(2 * GROUP_W, ffn),
            pl.BlockSpec((GROUP_W, 1), lambda l, j: (0, 0)),
        ],
        out_specs=[
            pl.BlockSpec((1, GROUP_W, cb), lambda l, j: (l, 0, j)),
            pl.BlockSpec((1, GROUP_W, LANES), lambda l, j: (l, 0, 0)),
        ],
        out_shape=[
            jax.ShapeDtypeStruct((depth, GROUP_W, 2 * seq), F32),
            jax.ShapeDtypeStruct((depth, GROUP_W, LANES), F32),
        ],
        compiler_params=_params("arbitrary", "arbitrary"),
        name="hyfilter",
    )(emb2, t2, w1t, col(hy_b1), col(hy_freq), w2t, col(hy_b2), w3t, deltas)
    return kk, nrm


def _swap16(y):
    lane = lax.broadcasted_iota(jnp.int32, y.shape, 1)
    first = (lane & 31) < 16
    return jnp.where(first, pltpu.roll(y, LANES - 16, 1), pltpu.roll(y, 16, 1))


def _group_mean_sq(p, gmat, gsize):
    x2 = p * p
    hi = x2.astype(BF16)
    lo = (x2 - hi.astype(F32)).astype(BF16)
    return (_dot(hi, gmat) + _dot(lo, gmat)) * (1.0 / gsize)


def _norm_rope(p, gain, gmat, gsize, cos, sin):
    ms = _group_mean_sq(p, gmat, gsize)
    y = p * lax.rsqrt(ms + EPS) * gain
    return y * cos + _swap16(y) * sin


def _store_values_with_ones(v_ref, slab, v):
    low = lax.broadcasted_iota(jnp.int32, v.shape, 1) < HEAD_DIM
    v_ref[:, slab * LANES:(slab + 1) * LANES] = jnp.where(low, v, 1.0).astype(BF16)
    v_ref[:, (slab + 1) * LANES:(slab + 2) * LANES] = jnp.where(
        low, pltpu.roll(v, HEAD_DIM, 1), 1.0).astype(BF16)


def _inproj_kernel(x_ref, mod_ref, ng_ref, w_ref, aqn_ref, akn_ref, cqn_ref, ckn_ref,
                   g64_ref, g32_ref, cosa_ref, sina_ref, cosc_ref, sinc_ref,
                   qa_ref, kat_ref, va_ref, ga_ref, bp_ref, gb_ref,
                   qc_ref, kct_ref, vc_ref, gc_ref, dbg_ref, du_ref, gd_ref, *, d_model):
    x = x_ref[...]
    mod = mod_ref[0]
    shift = mod[:, 0:d_model]
    scale = mod[:, d_model:2 * d_model]
    ms = jnp.mean(x * x, axis=-1, keepdims=True)
    h = x * lax.rsqrt(ms + EPS) * ng_ref[...]
    hb = (h * (1.0 + scale) + shift).astype(BF16)

    def proj(lo, width):
        return _dot(hb, w_ref[:, lo:lo + width])

    g64 = g64_ref[...]
    g32 = g32_ref[...]
    cosa, sina = cosa_ref[...], sina_ref[...]
    cosc, sinc = cosc_ref[...], sinc_ref[...]
    gw = GROUP_W
    sa = HEAD_DIM ** -0.5 * LOG2E
    sc = C_SUB ** -0.5 * LOG2E

    off = 0
    pq = proj(off, gw)
    for i in range(gw // LANES):
        q = _norm_rope(pq[:, i * LANES:(i + 1) * LANES], aqn_ref[...], g64, HEAD_DIM, cosa, sina)
        qa_ref[:, i * LANES:(i + 1) * LANES] = (q * sa).astype(BF16)
    off += gw
    pkv = proj(off, gw)
    kt = _norm_rope(pkv[:, 0:LANES], akn_ref[...], g64, HEAD_DIM, cosa, sina).T.astype(BF16)
    hd = HEAD_DIM
    kat_ref[0, 0, 0 * hd:1 * hd, :] = kt[0:hd]
    kat_ref[0, 0, 1 * hd:2 * hd, :] = kt[0:hd]
    kat_ref[0, 0, 2 * hd:3 * hd, :] = kt[hd:2 * hd]
    kat_ref[0, 0, 3 * hd:4 * hd, :] = kt[hd:2 * hd]
    _store_values_with_ones(va_ref, 0, pkv[:, LANES:2 * LANES])
    off += gw
    ga_ref[...] = proj(off, gw)
    off += gw
    bp_ref[...] = proj(off, 3 * gw)
    off += 3 * gw
    gb_ref[...] = proj(off, gw)
    off += gw
    pq = proj(off, gw)
    for i in range(gw // LANES):
        q = _norm_rope(pq[:, i * LANES:(i + 1) * LANES], cqn_ref[...], g32, C_SUB, cosc, sinc)
        qc_ref[:, i * LANES:(i + 1) * LANES] = (q * sc).astype(BF16)
    off += gw
    pk = proj(off, gw)
    for i in range(gw // LANES):
        k = _norm_rope(pk[:, i * LANES:(i + 1) * LANES], ckn_ref[...], g32, C_SUB, cosc, sinc)
        kct_ref[0, 0, i * LANES:(i + 1) * LANES, :] = k.T.astype(BF16)
    off += gw
    pv = proj(off, gw)
    for i in range(gw // LANES):
        _store_values_with_ones(vc_ref, 2 * i, pv[:, i * LANES:(i + 1) * LANES])
    off += gw
    gc_ref[...] = proj(off, gw)
    off += gw
    pd = proj(off, 3 * gw)
    dbg_ref[...] = pd[:, 0:gw]
    du_ref[...] = pd[:, gw:2 * gw] * pd[:, 2 * gw:3 * gw]
    off += 3 * gw
    gd_ref[...] = proj(off, gw)


def _group_matrix(gsize):
    i = jnp.arange(LANES) // gsize
    return (i[:, None] == i[None, :]).astype(BF16)


def _inproj(x2, mod3, norm_g, w_in_bf, a_qn, a_kn, c_qn, c_kn, tables, *, seq, tm):
    rows, d_model = x2.shape
    d_in = w_in_bf.shape[1]
    tps = seq // tm
    bsz = rows // seq
    gw = GROUP_W
    rep = lambda v, g: jnp.tile(v, LANES // g)[None, :]
    full = lambda r, c: pl.BlockSpec((r, c), lambda i: (0, 0))
    rowblk = lambda c: pl.BlockSpec((tm, c), lambda i: (i, 0))
    tab = pl.BlockSpec((tm, LANES), lambda i: (i % tps, 0))
    ktspec = lambda r: pl.BlockSpec((1, 1, r, tm), lambda i: (i // tps, i % tps, 0, 0))
    out_shape = [
        jax.ShapeDtypeStruct((rows, gw), BF16),
        jax.ShapeDtypeStruct((bsz, tps, 2 * LANES, tm), BF16),
        jax.ShapeDtypeStruct((rows, 2 * LANES), BF16),
        jax.ShapeDtypeStruct((rows, gw), F32),
        jax.ShapeDtypeStruct((rows, 3 * gw), F32),
        jax.ShapeDtypeStruct((rows, gw), F32),
        jax.ShapeDtypeStruct((rows, gw), BF16),
        jax.ShapeDtypeStruct((bsz, tps, gw, tm), BF16),
        jax.ShapeDtypeStruct((rows, 2 * gw), BF16),
        jax.ShapeDtypeStruct((rows, gw), F32),
        jax.ShapeDtypeStruct((rows, gw), F32),
        jax.ShapeDtypeStruct((rows, gw), F32),
        jax.ShapeDtypeStruct((rows, gw), F32),
    ]
    out_specs = [
        rowblk(gw), ktspec(2 * LANES), rowblk(2 * LANES), rowblk(gw), rowblk(3 * gw), rowblk(gw),
        rowblk(gw), ktspec(gw), rowblk(2 * gw), rowblk(gw), rowblk(gw), rowblk(gw), rowblk(gw),
    ]
    return pl.pallas_call(
        functools.partial(_inproj_kernel, d_model=d_model),
        grid=(rows // tm,),
        in_specs=[
            rowblk(d_model),
            pl.BlockSpec((1, 1, 3 * d_model), lambda i: (i // tps, 0, 0)),
            full(1, d_model),
            full(d_model, d_in),
            full(1, LANES), full(1, LANES), full(1, LANES), full(1, LANES),
            full(LANES, LANES), full(LANES, LANES),
            tab, tab, tab, tab,
        ],
        out_specs=out_specs,
        out_shape=out_shape,
        compiler_params=_params("arbitrary"),
        name="inproj",
    )(x2, mod3, norm_g[None, :], w_in_bf, rep(a_qn, HEAD_DIM), rep(a_kn, HEAD_DIM),
      rep(c_qn, C_SUB), rep(c_kn, C_SUB), _group_matrix(HEAD_DIM), _group_matrix(C_SUB), *tables)


def _stage(nc, kc, rows, qk=None, pv=None, mid=None):
    mrun = jnp.full((rows, LANES), NEG_BIG, F32)
    acc = jnp.zeros((rows, LANES), F32)
    for c in range(nc):
        if qk is not None:
            q2, kt_ref, krow, s_out = qk
            s = _dot(q2, kt_ref[0, c, krow:krow + LANES, :])
            s_out[c] = s
            for j in range(kc // LANES):
                mrun = jnp.maximum(mrun, s[:, j * LANES:(j + 1) * LANES])
        if pv is not None:
            s_in, mb, v_ref, vcol = pv
            s = s_in[c]
            pb = jnp.concatenate(
                [jnp.exp2(s[:, j * LANES:(j + 1) * LANES] - mb).astype(BF16)
                 for j in range(kc // LANES)], axis=1)
            acc = acc + _dot(pb, v_ref[c * kc:(c + 1) * kc, vcol:vcol + LANES])
        if mid is not None and c == nc // 2:
            mid()
    mb_new =jnp.broadcast_to(jnp.max(mrun, axis=-1, keepdims=True), (rows, LANES))
    return mb_new, acc


def _normalise(acc):
    return acc * (1.0 / pltpu.roll(acc, HEAD_DIM, 1))


def _init_pipeline(zeros=(), ones=()):
    @pl.when(pl.program_id(0) == 0)
    def _():
        for r in zeros:
            r[...] = jnp.zeros(r.shape, r.dtype)
        for r in ones:
            r[...] = jnp.ones(r.shape, r.dtype)


def _attn_a_kernel(q_ref, kt_ref, v_ref, vprev_ref, o0_ref, o1_ref, s0_ref, s1_ref, mb_ref, acc_ref,
                   *, nc, kc, tq):
    s_ref = (s0_ref, s1_ref)
    rows = 2 * tq
    low = lax.broadcasted_iota(jnp.int32, (tq, LANES), 1) < HEAD_DIM
    _init_pipeline(zeros=(s_ref[1], mb_ref), ones=(acc_ref,))

    def stack(h):
        qh = q_ref[:, h * LANES:(h + 1) * LANES]
        zero = jnp.zeros_like(qh)
        return jnp.concatenate([jnp.where(low, qh, zero), jnp.where(low, zero, qh)], axis=0)

    def finish(acc):
        o = _normalise(acc)
        return jnp.where(low, o[0:tq], pltpu.roll(o[tq:2 * tq], HEAD_DIM, 1))

    def finish_previous_0():
        o0_ref[...] = finish(acc_ref[...])

    mb0, acc1 = _stage(nc, kc, rows, qk=(stack(0), kt_ref, 0, s_ref[0]),
                       pv=(s_ref[1], mb_ref[...], vprev_ref, 0), mid=finish_previous_0)

    def finish_previous_1():
        o1_ref[...] = finish(acc1)

    mb1, acc0 = _stage(nc, kc, rows, qk=(stack(1), kt_ref, LANES, s_ref[1]),
                       pv=(s_ref[0], mb0, v_ref, 0), mid=finish_previous_1)
    mb_ref[...] = mb1
    acc_ref[...] = acc0


def _pipeline_specs(tq, seq, n_tiles, q_width, kt_rows, nc, kc, v_width, last_unit):
    nq = seq // tq
    cur = lambda i: jnp.minimum(i, n_tiles - 1)
    prev = lambda i: jnp.maximum(i - 1, 0)
    in_specs = [
        pl.BlockSpec((tq, q_width), lambda i: (cur(i), 0)),
        pl.BlockSpec((1, nc, kt_rows, kc), lambda i: (cur(i) // nq, 0, 0, 0)),
        pl.BlockSpec((seq, v_width), lambda i: (cur(i) // nq, 0)),
        pl.BlockSpec((seq, LANES), lambda i: (prev(i) // nq, last_unit)),
    ]
    out_cur = pl.BlockSpec((tq, LANES), lambda i: (cur(i), 0))
    out_prev = pl.BlockSpec((tq, LANES), lambda i: (prev(i), 0))
    return in_specs, out_cur, out_prev


def _attn_a(qa, kat, va, *, seq, tq):
    rows = qa.shape[0]
    _, nc, _, kc = kat.shape
    n_tiles = rows // tq
    in_specs, _, out_prev = _pipeline_specs(tq, seq, n_tiles, GROUP_W, 2 * LANES, nc, kc, 2 * LANES, 1)
    return pl.pallas_call(
        functools.partial(_attn_a_kernel, nc=nc, kc=kc, tq=tq),
        grid=(n_tiles + 1,),
        in_specs=in_specs,
        out_specs=[out_prev, out_prev],
        out_shape=[jax.ShapeDtypeStruct((rows, LANES), F32)] * 2,
        scratch_shapes=[pltpu.VMEM((nc, 2 * tq, kc), F32)] * 2 + [pltpu.VMEM((2 * tq, LANES), F32)] * 2,
        compiler_params=_params("arbitrary"),
        name="attn_a",
    )(qa, kat, va, va)


def _attn_c_kernel(q_ref, kt_ref, v_ref, vprev_ref, lam_ref, sub_ref, g64_ref, o0_ref, o1_ref,
                   s0_ref, s1_ref, s2_ref, s3_ref, mb_ref, acc2_ref, *, nc, kc, tq, lambda_init):
    s_ref = (s0_ref, s1_ref, s2_ref, s3_ref)
    rows = 2 * tq
    lamv = lam_ref[...]
    lam = (jnp.exp(jnp.sum(lamv[0:1] * lamv[1:2], axis=-1, keepdims=True))
           - jnp.exp(jnp.sum(lamv[2:3] * lamv[3:4], axis=-1, keepdims=True)) + lambda_init)
    lane = lax.broadcasted_iota(jnp.int32, (tq, LANES), 1)
    low = lane < HEAD_DIM
    _init_pipeline(zeros=(s_ref[3], mb_ref), ones=(acc2_ref,))

    def stack(u):
        qh = q_ref[:, (u // 2) * LANES:(u // 2 + 1) * LANES]
        zero = jnp.zeros_like(qh)
        base = (u % 2) * HEAD_DIM
        in1 = (lane >= base) & (lane < base + C_SUB)
        in2 = (lane >= base + C_SUB) & (lane < base + HEAD_DIM)
        return jnp.concatenate([jnp.where(in1, qh, zero), jnp.where(in2, qh, zero)], axis=0)

    def qk(u):
        return (stack(u), kt_ref, (u // 2) * LANES, s_ref[u])

    def pv(u, mb):
        return (s_ref[u], mb, v_ref, u * LANES)

    def differential(acc):
        o = _normalise(acc)
        return o[0:tq] - lam * o[tq:2 * tq]

    def sub_norm(d_even, d_odd):
        oi = jnp.where(low, d_even, pltpu.roll(d_odd, HEAD_DIM, 1))
        ms = _group_mean_sq(oi, g64_ref[...], HEAD_DIM)
        return oi * lax.rsqrt(ms + EPS) * sub_ref[...] * (1.0 - lambda_init)

    mb0, acc3 = _stage(nc, kc, rows, qk=qk(0), pv=(s_ref[3], mb_ref[...], vprev_ref, 0))

    def finish_previous():
        o1_ref[...] = sub_norm(differential(acc2_ref[...]), differential(acc3))

    mb1, acc0 = _stage(nc, kc, rows, qk=qk(1), pv=pv(0, mb0), mid=finish_previous)
    mb2, acc1 = _stage(nc, kc, rows, qk=qk(2), pv=pv(1, mb1))

    def finish_current():
        o0_ref[...] = sub_norm(differential(acc0), differential(acc1))

    mb3, acc2 = _stage(nc, kc, rows, qk=qk(3), pv=pv(2, mb2), mid=finish_current)
    mb_ref[...] = mb3
    acc2_ref[...] = acc2


def _attn_c(qc, kct, vc, lamv, c_subln, *, seq, tq, lambda_init):
    rows = qc.shape[0]
    _, nc, _, kc = kct.shape
    n_tiles = rows // tq
    full = lambda r, c: pl.BlockSpec((r, c), lambda i: (0, 0))
    in_specs, out_cur, out_prev = _pipeline_specs(
        tq, seq, n_tiles, GROUP_W, GROUP_W, nc, kc, 2 * GROUP_W, 3)
    return pl.pallas_call(
        functools.partial(_attn_c_kernel, nc=nc, kc=kc, tq=tq, lambda_init=lambda_init),
        grid=(n_tiles + 1,),
        in_specs=in_specs + [full(4, C_SUB), full(1, LANES), full(LANES, LANES)],
        out_specs=[out_cur, out_prev],
        out_shape=[jax.ShapeDtypeStruct((rows, LANES), F32)] * 2,
        scratch_shapes=[pltpu.VMEM((nc, 2 * tq, kc), F32)] * (GROUP_W // HEAD_DIM)
        + [pltpu.VMEM((2 * tq, LANES), F32)] * 2,
        compiler_params=_params("arbitrary"),
        name="attn_c",
    )(qc, kct, vc, vc, lamv, jnp.tile(c_subln, LANES // HEAD_DIM)[None, :], _group_matrix(HEAD_DIM))


def _neighbours(u, prev_ref, next_ref, tps):
    tm = u.shape[0]
    i = pl.program_id(0)
    has_prev = (i % tps != 0).astype(F32)
    has_next = (i % tps != tps - 1).astype(F32)
    prev_row = prev_ref[SUBLANES - 1:SUBLANES, :] * has_prev
    next_row = next_ref[0:1, :] * has_next
    rid = lax.broadcasted_iota(jnp.int32, u.shape, 0)
    up = jnp.where(rid == 0, prev_row, pltpu.roll(u, 1, 0))
    dn = jnp.where(rid == tm - 1, next_row, pltpu.roll(u, tm - 1, 0))
    return up, dn


def _halo_specs(tm, width, nrows):
    per = tm // SUBLANES
    last = nrows // SUBLANES - 1
    prev = pl.BlockSpec((SUBLANES, width), lambda i: (jnp.maximum(i * per - 1, 0), 0))
    nxt = pl.BlockSpec((SUBLANES, width), lambda i: (jnp.minimum((i + 1) * per, last), 0))
    return prev, nxt


def _hypre_kernel(bp_ref, prev_ref, next_ref, w_ref, b_ref, x0_ref, z_ref, *, tps):
    u = bp_ref[...]
    up, dn = _neighbours(u, prev_ref, next_ref, tps)
    w = w_ref[...]
    y = up * w[0:1] + u * w[1:2] + dn * w[2:3] + b_ref[...]
    gw = GROUP_W
    x0_ref[...] = y[:, 0:gw]
    z_ref[...] = y[:, gw:2 * gw] * y[:, 2 * gw:3 * gw]


def _hypre(bp, conv_w, conv_b, *, seq, tm):
    rows, width = bp.shape
    prev, nxt = _halo_specs(tm, width, rows)
    rowblk = lambda c: pl.BlockSpec((tm, c), lambda i: (i, 0))
    full = lambda r, c: pl.BlockSpec((r, c), lambda i: (0, 0))
    return pl.pallas_call(
        functools.partial(_hypre_kernel, tps=seq // tm),
        grid=(rows // tm,),
        in_specs=[rowblk(width), prev, nxt, full(3, width), full(1, width)],
        out_specs=[rowblk(GROUP_W), rowblk(GROUP_W)],
        out_shape=[jax.ShapeDtypeStruct((rows, GROUP_W), F32)] * 2,
        compiler_params=_params("arbitrary"),
        name="hypre",
    )(bp, bp, bp, conv_w, conv_b[None, :])


HY_CH = 4


def _hyconv_kernel(z_ref, kk_ref, nrm_ref, o_ref, acc_ref, *, nb, bsz):
    for ch in range(z_ref.shape[0]):
        _hyconv_channel(z_ref.at[ch], kk_ref.at[ch], nrm_ref.at[ch], o_ref.at[ch], acc_ref.at[ch],
                        nb, bsz)


def _hyconv_channel(z_ref, kk_ref, nrm_ref, o_ref, acc_ref, nb, bsz):
    z = z_ref[...].reshape(nb * bsz, LANES)
    kk = kk_ref[...] * (1.0 / nrm_ref[...])
    upper = (lax.broadcasted_iota(jnp.int32, (LANES, LANES), 1)
             >= lax.broadcasted_iota(jnp.int32, (LANES, LANES), 0))
    rolled = {}

    def circulant(m):
        if m not in rolled:
            rolled[m] = pltpu.roll(jnp.broadcast_to(kk[m:m + 1], (LANES, LANES)), 0, 1,
                                   stride=1, stride_axis=0)
        return rolled[m]

    def toeplitz(d):
        return jnp.where(upper, circulant(d + nb), circulant(d + nb - 1)).astype(BF16)

    acc_ref[...] = jnp.zeros_like(acc_ref)
    for d in range(0, nb, 2):
        m = bsz * (nb - d)
        out = _dot(z[0:m].astype(BF16), jnp.concatenate([toeplitz(d), toeplitz(d + 1)], axis=1))
        acc_ref[bsz * d:bsz * d + m, :] += out[:, 0:LANES]
        acc_ref[bsz * (d + 1):bsz * d + m, :] += out[0:m - bsz, LANES:2 * LANES]
    for e in range(1, nb - 1, 2):
        m = bsz * (nb - e)
        out = _dot(z[bsz * e:bsz * e + m].astype(BF16),
                   jnp.concatenate([toeplitz(-e), toeplitz(-e - 1)], axis=1))
        acc_ref[0:m, :] += out[:, 0:LANES]
        acc_ref[0:m - bsz, :] += out[bsz:m, LANES:2 * LANES]
    e = nb - 1
    acc_ref[0:bsz, :] += _dot(z[bsz * e:bsz * nb].astype(BF16), toeplitz(-e))
    o_ref[...] = acc_ref[...].reshape(nb, bsz, LANES)


def _hyconv(zt, kk3, nrm3):
    ch, nb, bsz, _ = zt.shape
    assert nb % 2 == 0 and ch % HY_CH == 0
    return pl.pallas_call(
        functools.partial(_hyconv_kernel, nb=nb, bsz=bsz),
        grid=(ch // HY_CH,),
        in_specs=[
            pl.BlockSpec((HY_CH, nb, bsz, LANES), lambda c: (c, 0, 0, 0)),
            pl.BlockSpec((HY_CH, 2 * nb, LANES), lambda c: (c, 0, 0)),
            pl.BlockSpec((HY_CH, 1, LANES), lambda c: (c, 0, 0)),
        ],
        out_specs=pl.BlockSpec((HY_CH, nb, bsz, LANES), lambda c: (c, 0, 0, 0)),
        out_shape=jax.ShapeDtypeStruct(zt.shape, F32),
        scratch_shapes=[pltpu.VMEM((HY_CH, nb * bsz, LANES), F32)],
        compiler_params=_params("arbitrary"),
        name="hyconv",
    )(zt, kk3, nrm3)


def _outproj_kernel(x_ref, mod_ref, oa0_ref, oa1_ref, ga_ref, x0_ref, yb_ref, z_ref, gb_ref, hb_ref,
                    oc0_ref, oc1_ref, gc_ref, dbg_ref, du_ref, dprev_ref, dnext_ref, gd_ref, scw_ref,
                    w_ref, o_ref, *, tps, d_model):
    gate = mod_ref[0][:, 2 * d_model:3 * d_model]
    y_a = _silu(ga_ref[...]) * jnp.concatenate([oa0_ref[...], oa1_ref[...]], axis=1)
    z = z_ref[...]
    y_b = _silu(gb_ref[...]) * (x0_ref[...] * (yb_ref[...] + hb_ref[...] * z))
    y_c = _silu(gc_ref[...]) * jnp.concatenate([oc0_ref[...], oc1_ref[...]], axis=1)
    u = du_ref[...]
    up, dn = _neighbours(u, dprev_ref, dnext_ref, tps)
    w = scw_ref[...]
    y_d = _silu(gd_ref[...]) * (dbg_ref[...] * (up * w[0:1] + u * w[1:2] + dn * w[2:3]))
    cat = jnp.concatenate([y_a, y_b, y_c, y_d], axis=1).astype(BF16)
    o_ref[...] = x_ref[...] + gate * _dot(cat, w_ref[...])


def _outproj(x2, mod3, oa, ga, x0, yb, z, gb, hy_bias, oc, gc, dbg, du, gd, sc_conv_w, w_out_bf,
             *, seq, tm):
    rows, d_model = x2.shape
    tps = seq // tm
    gw = GROUP_W
    rowblk = lambda c: pl.BlockSpec((tm, c), lambda i: (i, 0))
    full = lambda r, c: pl.BlockSpec((r, c), lambda i: (0, 0))
    prev, nxt = _halo_specs(tm, gw, rows)
    g = rowblk(gw)
    hf = rowblk(LANES)
    return pl.pallas_call(
        functools.partial(_outproj_kernel, tps=tps, d_model=d_model),
        grid=(rows // tm,),
        in_specs=[
            rowblk(d_model),
            pl.BlockSpec((1, 1, 3 * d_model), lambda i: (i // tps, 0, 0)),
            hf, hf, g, g, g, g, g, full(1, gw), hf, hf, g, g, g, prev, nxt, g, full(3, gw),
            full(w_out_bf.shape[0], d_model),
        ],
        out_specs=rowblk(d_model),
        out_shape=jax.ShapeDtypeStruct((rows, d_model), F32),
        compiler_params=_params("arbitrary"),
        name="outproj",
    )(x2, mod3, *oa, ga, x0, yb, z, gb, hy_bias[None, :], *oc, gc, dbg, du, du, du, gd,
      sc_conv_w, w_out_bf)


def _rope_tables(seq):
    def cos_sin(pos, dim):
        inv = ROPE_THETA ** (-jnp.arange(0, dim, 2, dtype=F32) / dim)
        ang = pos.astype(F32)[:, None] * inv[None, :]
        return jnp.cos(ang), jnp.sin(ang)

    t = jnp.arange(seq, dtype=jnp.int32)
    cr, sr = cos_sin(t // GRID_W, HEAD_DIM // 2)
    cc, sc = cos_sin(t % GRID_W, HEAD_DIM // 2)
    cq, sq = cos_sin(t, C_SUB)
    cosa = jnp.tile(jnp.concatenate([cr, cr, cc, cc], axis=1), (1, LANES // HEAD_DIM))
    sina = jnp.tile(jnp.concatenate([-sr, sr, -sc, sc], axis=1), (1, LANES // HEAD_DIM))
    cosc = jnp.tile(jnp.concatenate([cq, cq], axis=1), (1, LANES // C_SUB))
    sinc = jnp.tile(jnp.concatenate([-sq, sq], axis=1), (1, LANES // C_SUB))
    return cosa, sina, cosc, sinc


def _tile_rows(seq, want):
    return want if seq % want == 0 else seq


def kernel(x, c, norm_g, w_ada, b_ada, w_in, w_out, a_qn, a_kn, hy_conv_w, hy_conv_b, hy_w1, hy_b1,
           hy_freq, hy_w2, hy_b2, hy_w3, hy_bias, c_qn, c_kn, lam_q1, lam_k1, lam_q2, lam_k2,
           c_subln, sc_conv_w):
    bsz, seq, d_model = x.shape
    depth = w_in.shape[0]
    nb = seq // LANES
    tm = _tile_rows(seq, 512)
    tq = _tile_rows(seq, 256)

    tables = _rope_tables(seq)
    mod = _ada(c, w_ada, b_ada)
    kk, nrm = _hyfilter(seq, hy_w1, hy_b1, hy_freq, hy_w2, hy_b2, hy_w3)
    w_in_bf = w_in.astype(BF16)
    w_out_bf = w_out.astype(BF16)

    x2 = x.reshape(bsz * seq, d_model)
    for l in range(depth):
        mod3 = mod[l][:, None, :]
        (qa, kat, va, ga, bp, gb, qc, kct, vc, gc, dbg, du, gd) = _inproj(
            x2, mod3, norm_g[l], w_in_bf[l], a_qn[l], a_kn[l], c_qn[l], c_kn[l], tables,
            seq=seq, tm=tm)
        oa = _attn_a(qa, kat, va, seq=seq, tq=tq)
        lambda_init = 0.8 - 0.6 * math.exp(-0.3 * l)
        lamv = jnp.stack([lam_q1[l], lam_k1[l], lam_q2[l], lam_k2[l]])
        oc = _attn_c(qc, kct, vc, lamv, c_subln[l], seq=seq, tq=tq, lambda_init=lambda_init)
        x0, z = _hypre(bp, hy_conv_w[l], hy_conv_b[l], seq=seq, tm=tm)
        zt = z.reshape(bsz, nb, LANES, GROUP_W).transpose(3, 1, 0, 2)
        yt = _hyconv(zt, kk[l].reshape(GROUP_W, 2 * nb, LANES), nrm[l][:, None, :])
        yb = yt.transpose(2, 1, 3, 0).reshape(bsz * seq, GROUP_W)
        x2 = _outproj(x2, mod3, oa, ga, x0, yb, z, gb, hy_bias[l], oc, gc, dbg, du, gd,
                      sc_conv_w[l], w_out_bf[l], seq=seq, tm=tm)
    return x2.reshape(bsz, seq, d_model)
```

```python
import functools
import math

import jax
import jax.numpy as jnp
from jax import lax
from jax.experimental import pallas as pl
from jax.experimental.pallas import tpu as pltpu

F32 = jnp.float32
BF16 = jnp.bfloat16

GROUP_W = 256
HEAD_DIM = 64
C_SUB = 32
HY_EMB = 33
HY_BANDS = (HY_EMB - 1) // 2
HY_SHIFT = 0.05
HY_FAST = 0.3
HY_SLOW = 1.5
HY_TARGET = 1e-2
GRID_W = 64
ROPE_THETA = 10000.0
EPS = 1e-6
LOG2E = 1.4426950408889634

LANES = 128
SUBLANES = 8
VMEM_LIMIT = 56 * 1024 * 1024
NEG_BIG = -3.0e38

HIGHEST = lax.Precision.HIGHEST


def _params(*sem):
    return pltpu.CompilerParams(dimension_semantics=sem, vmem_limit_bytes=VMEM_LIMIT)


def _dot(a, b):
    return jnp.dot(a, b, preferred_element_type=F32)


def _dot_exact(a, b):
    return jnp.dot(a, b, preferred_element_type=F32, precision=HIGHEST)


def _silu(x):
    return x * (1.0 / (1.0 + jnp.exp(-x)))


def _ada_kernel(c_ref, w_ref, b_ref, o_ref):
    o_ref[0] = _dot_exact(_silu(c_ref[...]), w_ref[0]) + b_ref[0]


def _ada(c, w_ada, b_ada):
    depth, d, d3 = w_ada.shape
    bsz = c.shape[0]
    nb = d3 // d
    return pl.pallas_call(
        _ada_kernel,
        grid=(depth, nb),
        in_specs=[
            pl.BlockSpec((bsz, d), lambda l, j: (0, 0)),
            pl.BlockSpec((1, d, d), lambda l, j: (l, 0, j)),
            pl.BlockSpec((1, 1, d), lambda l, j: (l, 0, j)),
        ],
        out_specs=pl.BlockSpec((1, bsz, d), lambda l, j: (l, 0, j)),
        out_shape=jax.ShapeDtypeStruct((depth, bsz, d3), F32),
        compiler_params=_params("arbitrary", "arbitrary"),
        name="ada",
    )(c, w_ada, b_ada.reshape(depth, 1, d3))


def _hyfilter_kernel(emb_ref, t_ref, w1_ref, b1_ref, fr_ref, w2_ref, b2_ref, w3_ref, dl_ref,
                     kk_ref, nrm_ref, *, seq, cb):
    j = pl.program_id(1)
    fr = fr_ref[0]
    h = jnp.sin(fr * (_dot_exact(w1_ref[0], emb_ref[...]) + b1_ref[0]))
    h = jnp.sin(fr * (_dot_exact(w2_ref[0], h) + b2_ref[0]))
    h = _dot_exact(w3_ref[0], h)
    window = jnp.exp(-t_ref[...] * dl_ref[...]) + HY_SHIFT
    col = j * cb + lax.broadcasted_iota(jnp.int32, (GROUP_W, cb), 1)
    kk = jnp.where(col >= seq, h[:GROUP_W], h[GROUP_W:]) * window
    kk = jnp.where(col == 0, 0.0, kk)
    kk_ref[0] = kk

    part = jnp.abs(kk[:, 0:LANES])
    for i in range(1, cb // LANES):
        part = part + jnp.abs(kk[:, i * LANES:(i + 1) * LANES])

    @pl.when(j == 0)
    def _():
        nrm_ref[0] = part

    @pl.when(j > 0)
    def _():
        nrm_ref[0] = nrm_ref[0] + part

    @pl.when(j == pl.num_programs(1) - 1)
    def _():
        tot = jnp.sum(nrm_ref[0], axis=-1, keepdims=True)
        nrm_ref[0] = jnp.broadcast_to(tot, (GROUP_W, LANES))


def _hyfilter(seq, hy_w1, hy_b1, hy_freq, hy_w2, hy_b2, hy_w3):
    depth = hy_w1.shape[0]
    ffn = hy_w1.shape[2]
    t = jnp.linspace(0.0, 1.0, seq, dtype=F32)[:, None]
    w = 2.0 * math.pi * jnp.arange(seq, dtype=F32)[:, None] / seq
    f = jnp.linspace(1e-4, HY_BANDS - 1, HY_BANDS, dtype=F32)[None, :]
    emb = jnp.concatenate([t, jnp.cos(f * w), -jnp.sin(f * w)], axis=-1)
    both = lambda v: jnp.concatenate([v[0:1], v[:0:-1], v], axis=0)
    emb2 = jnp.pad(both(emb).T, ((0, LANES - HY_EMB), (0, 0)))
    t2 = both(t).T
    max_decay = math.log(HY_TARGET) / HY_FAST
    min_decay = math.log(HY_TARGET) / HY_SLOW
    deltas = jnp.abs(jnp.linspace(min_decay, max_decay, GROUP_W, dtype=F32))[:, None]

    w1t = jnp.pad(jnp.swapaxes(hy_w1, 1, 2), ((0, 0), (0, 0), (0, LANES - HY_EMB)))
    w2t = jnp.swapaxes(hy_w2, 1, 2)
    w3t = jnp.swapaxes(hy_w3, 1, 2)
    col = lambda v: v[:, :, None]
    cb = min(1024, 2 * seq)
    nblk = (2 * seq) // cb
    wspec = lambda r, c: pl.BlockSpec((1, r, c), lambda l, j: (l, 0, 0))
    kk, nrm = pl.pallas_call(
        functools.partial(_hyfilter_kernel, seq=seq, cb=cb),
        grid=(depth, nblk),
        in_specs=[
            pl.BlockSpec((LANES, cb), lambda l, j: (0, j)),
            pl.BlockSpec((1, cb), lambda l, j: (0, j)),
            wspec(ffn, LANES), wspec(ffn, 1), wspec(ffn, 1), wspec(ffn, ffn), wspec(ffn, 1),
            wspec(2 * GROUP_W, ffn),
            pl.BlockSpec((GROUP_W, 1), lambda l, j: (0, 0)),
        ],
        out_specs=[
            pl.BlockSpec((1, GROUP_W, cb), lambda l, j: (l, 0, j)),
            pl.BlockSpec((1, GROUP_W, LANES), lambda l, j: (l, 0, 0)),
        ],
        out_shape=[
            jax.ShapeDtypeStruct((depth, GROUP_W, 2 * seq), F32),
            jax.ShapeDtypeStruct((depth, GROUP_W, LANES), F32),
        ],
        compiler_params=_params("arbitrary", "arbitrary"),
        name="hyfilter",
    )(emb2, t2, w1t, col(hy_b1), col(hy_freq), w2t, col(hy_b2), w3t, deltas)
    return kk, nrm


def _swap16(y):
    lane = lax.broadcasted_iota(jnp.int32, y.shape, 1)
    first = (lane & 31) < 16
    return jnp.where(first, pltpu.roll(y, LANES - 16, 1), pltpu.roll(y, 16, 1))


def _group_mean_sq(p, gmat, gsize):
    return _dot((p * p).astype(BF16), gmat) * (1.0 / gsize)


def _norm_rope(p, gain, gmat, gsize, cos, sin):
    ms = _group_mean_sq(p, gmat, gsize)
    y = p * lax.rsqrt(ms + EPS) * gain
    return y * cos + _swap16(y) * sin


def _store_values_with_ones(v_ref, slab, v):
    low = lax.broadcasted_iota(jnp.int32, v.shape, 1) < HEAD_DIM
    v_ref[:, slab * LANES:(slab + 1) * LANES] = jnp.where(low, v, 1.0).astype(BF16)
    v_ref[:, (slab + 1) * LANES:(slab + 2) * LANES] = jnp.where(
        low, pltpu.roll(v, HEAD_DIM, 1), 1.0).astype(BF16)


def _conv3(main, prev_row, next_row, w):
    tm = main.shape[0]
    rid = lax.broadcasted_iota(jnp.int32, main.shape, 0)
    up = jnp.where(rid == 0, prev_row, pltpu.roll(main, 1, 0))
    dn = jnp.where(rid == tm - 1, next_row, pltpu.roll(main, tm - 1, 0))
    return up * w[0:1] + main * w[1:2] + dn * w[2:3]


def _inproj_kernel(x_ref, xprev_ref, xnext_ref, mod_ref, ng_ref, w_ref, aqn_ref, akn_ref, cqn_ref,
                   ckn_ref, g64_ref, g32_ref, cosa_ref, sina_ref, cosc_ref, sinc_ref,
                   hyw_ref, hyb_ref, scw_ref,
                   qa_ref, kat_ref, va_ref, ga_ref, gx0_ref, z_ref,
                   qc_ref, kct_ref, vc_ref, gc_ref, yd_ref, *, d_model, tps):
    tm = x_ref.shape[0]
    i = pl.program_id(0)
    has_prev = (i % tps != 0).astype(F32)
    has_next = (i % tps != tps - 1).astype(F32)
    x = jnp.concatenate([x_ref[...], xprev_ref[...], xnext_ref[...]], axis=0)
    prev_at, next_at = tm + SUBLANES - 1, tm + SUBLANES
    mod = mod_ref[0]
    shift = mod[:, 0:d_model]
    scale = mod[:, d_model:2 * d_model]
    ms = jnp.mean(x * x, axis=-1, keepdims=True)
    h = x * lax.rsqrt(ms + EPS) * ng_ref[...]
    hb_ext = (h * (1.0 + scale) + shift).astype(BF16)
    hb = hb_ext[0:tm]

    def proj(lo, width):
        return _dot(hb, w_ref[:, lo:lo + width])

    def proj_with_neighbours(lo, width):
        return _dot(hb_ext, w_ref[:, lo:lo + width])

    g64 = g64_ref[...]
    g32 = g32_ref[...]
    cosa, sina = cosa_ref[...], sina_ref[...]
    cosc, sinc = cosc_ref[...], sinc_ref[...]
    gw = GROUP_W
    sa = HEAD_DIM ** -0.5 * LOG2E
    sc = C_SUB ** -0.5 * LOG2E

    hd = HEAD_DIM

    def norm_rope_a(p, gain_ref):
        return _norm_rope(p, gain_ref[...], g64, HEAD_DIM, cosa, sina)

    def norm_rope_c(p, gain_ref):
        return _norm_rope(p, gain_ref[...], g32, C_SUB, cosc, sinc)

    def a_query(p):
        for i in range(gw // LANES):
            q = norm_rope_a(p[:, i * LANES:(i + 1) * LANES], aqn_ref)
            qa_ref[:, i * LANES:(i + 1) * LANES] = (q * sa).astype(BF16)

    def a_key_value(p):
        kt = norm_rope_a(p[:, 0:LANES], akn_ref).T.astype(BF16)
        kat_ref[0, 0, 0 * hd:1 * hd, :] = kt[0:hd]
        kat_ref[0, 0, 1 * hd:2 * hd, :] = kt[0:hd]
        kat_ref[0, 0, 2 * hd:3 * hd, :] = kt[hd:2 * hd]
        kat_ref[0, 0, 3 * hd:4 * hd, :] = kt[hd:2 * hd]
        _store_values_with_ones(va_ref, 0, p[:, LANES:2 * LANES])

    def a_gate(p):
        ga_ref[...] = p

    held = {}

    def b_conv(p):
        y = _conv3(p[0:tm], p[prev_at:prev_at + 1] * has_prev, p[next_at:next_at + 1] * has_next,
                   hyw_ref[...]) + hyb_ref[...]
        z_ref[...] = y[:, gw:2 * gw] * y[:, 2 * gw:3 * gw]
        held["x0"] = y[:, 0:gw]

    def b_gate(p):
        gx0_ref[...] = _silu(p) * held["x0"]

    def c_query(p):
        for i in range(gw // LANES):
            q = norm_rope_c(p[:, i * LANES:(i + 1) * LANES], cqn_ref)
            qc_ref[:, i * LANES:(i + 1) * LANES] = (q * sc).astype(BF16)

    def c_key(p):
        for i in range(gw // LANES):
            k = norm_rope_c(p[:, i * LANES:(i + 1) * LANES], ckn_ref)
            kct_ref[0, 0, i * LANES:(i + 1) * LANES, :] = k.T.astype(BF16)

    def c_value(p):
        for i in range(gw // LANES):
            _store_values_with_ones(vc_ref, 2 * i, p[:, i * LANES:(i + 1) * LANES])

    def c_gate(p):
        gc_ref[...] = p

    def d_conv(p):
        u = p[:, gw:2 * gw] * p[:, 2 * gw:3 * gw]
        cv = _conv3(u[0:tm], u[prev_at:prev_at + 1] * has_prev, u[next_at:next_at + 1] * has_next,
                    scw_ref[...])
        held["d"] = p[0:tm, 0:gw] * cv

    def d_gate(p):
        yd_ref[...] = (_silu(p) * held["d"]).astype(BF16)

    groups = [
        (gw, False, a_query), (gw, False, a_key_value), (gw, False, a_gate),
        (3 * gw, True, b_conv), (gw, False, b_gate),
        (gw, False, c_query), (gw, False, c_key), (gw, False, c_value), (gw, False, c_gate),
        (3 * gw, True, d_conv), (gw, False, d_gate),
    ]
    off = 0
    pending = None
    for width, ext, epilogue in groups:
        p = (proj_with_neighbours if ext else proj)(off, width)
        off += width
        if pending is not None:
            pending[0](pending[1])
        pending = (epilogue, p)
    pending[0](pending[1])


def _group_matrix(gsize):
    i = jnp.arange(LANES) // gsize
    return (i[:, None] == i[None, :]).astype(BF16)


def _halo_specs(tm, width, nrows):
    per = tm // SUBLANES
    last = nrows // SUBLANES - 1
    prev = pl.BlockSpec((SUBLANES, width), lambda i: (jnp.maximum(i * per - 1, 0), 0))
    nxt = pl.BlockSpec((SUBLANES, width), lambda i: (jnp.minimum((i + 1) * per, last), 0))
    return prev, nxt


def _inproj(x2, mod3, norm_g, w_in_bf, a_qn, a_kn, c_qn, c_kn, tables, hy_conv_w, hy_conv_b,
            sc_conv_w, *, seq, tm):
    rows, d_model = x2.shape
    d_in = w_in_bf.shape[1]
    tps = seq // tm
    bsz = rows // seq
    gw = GROUP_W
    rep = lambda v, g: jnp.tile(v, LANES // g)[None, :]
    full = lambda r, c: pl.BlockSpec((r, c), lambda i: (0, 0))
    rowblk = lambda c: pl.BlockSpec((tm, c), lambda i: (i, 0))
    tab = pl.BlockSpec((tm, LANES), lambda i: (i % tps, 0))
    ktspec = lambda r: pl.BlockSpec((1, 1, r, tm), lambda i: (i // tps, i % tps, 0, 0))
    out_shape = [
        jax.ShapeDtypeStruct((rows, gw), BF16),
        jax.ShapeDtypeStruct((bsz, tps, 2 * LANES, tm), BF16),
        jax.ShapeDtypeStruct((rows, 2 * LANES), BF16),
        jax.ShapeDtypeStruct((rows, gw), F32),
        jax.ShapeDtypeStruct((rows, gw), F32),
        jax.ShapeDtypeStruct((rows, gw), F32),
        jax.ShapeDtypeStruct((rows, gw), BF16),
        jax.ShapeDtypeStruct((bsz, tps, gw, tm), BF16),
        jax.ShapeDtypeStruct((rows, 2 * gw), BF16),
        jax.ShapeDtypeStruct((rows, gw), F32),
        jax.ShapeDtypeStruct((rows, gw), BF16),
    ]
    out_specs = [
        rowblk(gw), ktspec(2 * LANES), rowblk(2 * LANES), rowblk(gw), rowblk(gw), rowblk(gw),
        rowblk(gw), ktspec(gw), rowblk(2 * gw), rowblk(gw), rowblk(gw),
    ]
    xprev, xnext = _halo_specs(tm, d_model, rows)
    return pl.pallas_call(
        functools.partial(_inproj_kernel, d_model=d_model, tps=tps),
        grid=(rows // tm,),
        in_specs=[
            rowblk(d_model), xprev, xnext,
            pl.BlockSpec((1, 1, 3 * d_model), lambda i: (i // tps, 0, 0)),
            full(1, d_model),
            full(d_model, d_in),
            full(1, LANES), full(1, LANES), full(1, LANES), full(1, LANES),
            full(LANES, LANES), full(LANES, LANES),
            tab, tab, tab, tab,
            full(3, 3 * gw), full(1, 3 * gw), full(3, gw),
        ],
        out_specs=out_specs,
        out_shape=out_shape,
        compiler_params=_params("arbitrary"),
        name="inproj",
    )(x2, x2, x2, mod3, norm_g[None, :], w_in_bf, rep(a_qn, HEAD_DIM), rep(a_kn, HEAD_DIM),
      rep(c_qn, C_SUB), rep(c_kn, C_SUB), _group_matrix(HEAD_DIM), _group_matrix(C_SUB), *tables,
      hy_conv_w, hy_conv_b[None, :], sc_conv_w)


def _stage(nc, kc, rows, qk=None, pv=None, mid=None):
    mrun = jnp.full((rows, LANES), NEG_BIG, F32)
    acc = jnp.zeros((rows, LANES), F32)
    for c in range(nc):
        if qk is not None:
            q2, kt_ref, krow, s_out = qk
            s = _dot(q2, kt_ref[0, c, krow:krow + LANES, :])
            s_out[c] = s
            for j in range(kc // LANES):
                mrun = jnp.maximum(mrun, s[:, j * LANES:(j + 1) * LANES])
        if pv is not None:
            s_in, mb, v_ref, vcol = pv
            s = s_in[c]
            pb = jnp.concatenate(
                [jnp.exp2(s[:, j * LANES:(j + 1) * LANES] - mb).astype(BF16)
                 for j in range(kc // LANES)], axis=1)
            acc = acc + _dot(pb, v_ref[c * kc:(c + 1) * kc, vcol:vcol + LANES])
        if mid is not None and c == nc // 2:
            mid()
    mb_new =jnp.broadcast_to(jnp.max(mrun, axis=-1, keepdims=True), (rows, LANES))
    return mb_new, acc


def _normalise(acc):
    return acc * (1.0 / pltpu.roll(acc, HEAD_DIM, 1))


def _init_pipeline(zeros=(), ones=()):
    @pl.when(pl.program_id(0) == 0)
    def _():
        for r in zeros:
            r[...] = jnp.zeros(r.shape, r.dtype)
        for r in ones:
            r[...] = jnp.ones(r.shape, r.dtype)


def _attn_a_kernel(q_ref, kt_ref, v_ref, vprev_ref, g0_ref, g1_ref, o0_ref, o1_ref,
                   s0_ref, s1_ref, mb_ref, acc_ref, *, nc, kc, tq):
    s_ref = (s0_ref, s1_ref)
    rows = 2 * tq
    low = lax.broadcasted_iota(jnp.int32, (tq, LANES), 1) < HEAD_DIM
    _init_pipeline(zeros=(s_ref[1], mb_ref), ones=(acc_ref,))

    def stack(h):
        qh = q_ref[:, h * LANES:(h + 1) * LANES]
        zero = jnp.zeros_like(qh)
        return jnp.concatenate([jnp.where(low, qh, zero), jnp.where(low, zero, qh)], axis=0)

    def finish(acc, g_ref):
        o = _normalise(acc)
        o = jnp.where(low, o[0:tq], pltpu.roll(o[tq:2 * tq], HEAD_DIM, 1))
        return (_silu(g_ref[...]) * o).astype(BF16)

    def finish_previous_0():
        o0_ref[...] = finish(acc_ref[...], g0_ref)

    mb0, acc1 = _stage(nc, kc, rows, qk=(stack(0), kt_ref, 0, s_ref[0]),
                       pv=(s_ref[1], mb_ref[...], vprev_ref, 0), mid=finish_previous_0)

    def finish_previous_1():
        o1_ref[...] = finish(acc1, g1_ref)

    mb1, acc0 = _stage(nc, kc, rows, qk=(stack(1), kt_ref, LANES, s_ref[1]),
                       pv=(s_ref[0], mb0, v_ref, 0), mid=finish_previous_1)
    mb_ref[...] = mb1
    acc_ref[...] = acc0


def _pipeline_specs(tq, seq, n_tiles, q_width, kt_rows, nc, kc, v_width, last_unit):
    nq = seq // tq
    cur = lambda i: jnp.minimum(i, n_tiles - 1)
    prev = lambda i: jnp.maximum(i - 1, 0)
    in_specs = [
        pl.BlockSpec((tq, q_width), lambda i: (cur(i), 0)),
        pl.BlockSpec((1, nc, kt_rows, kc), lambda i: (cur(i) // nq, 0, 0, 0)),
        pl.BlockSpec((seq, v_width), lambda i: (cur(i) // nq, 0)),
        pl.BlockSpec((seq, LANES), lambda i: (prev(i) // nq, last_unit)),
    ]
    tile_cur = lambda col: pl.BlockSpec((tq, LANES), lambda i: (cur(i), col))
    tile_prev = lambda col: pl.BlockSpec((tq, LANES), lambda i: (prev(i), col))
    return in_specs, tile_cur, tile_prev


def _attn_a(qa, kat, va, ga, *, seq, tq):
    rows = qa.shape[0]
    _, nc, _, kc = kat.shape
    n_tiles = rows // tq
    in_specs, _, tile_prev = _pipeline_specs(tq, seq, n_tiles, GROUP_W, 2 * LANES, nc, kc, 2 * LANES, 1)
    return pl.pallas_call(
        functools.partial(_attn_a_kernel, nc=nc, kc=kc, tq=tq),
        grid=(n_tiles + 1,),
        in_specs=in_specs + [tile_prev(0), tile_prev(1)],
        out_specs=[tile_prev(0), tile_prev(0)],
        out_shape=[jax.ShapeDtypeStruct((rows, LANES), BF16)] * 2,
        scratch_shapes=[pltpu.VMEM((nc, 2 * tq, kc), F32)] * 2 + [pltpu.VMEM((2 * tq, LANES), F32)] * 2,
        compiler_params=_params("arbitrary"),
        name="attn_a",
    )(qa, kat, va, va, ga, ga)


def _attn_c_kernel(q_ref, kt_ref, v_ref, vprev_ref, g0_ref, g1_ref, lam_ref, sub_ref, g64_ref,
                   o0_ref, o1_ref, s0_ref, s1_ref, s2_ref, s3_ref, mb_ref, acc2_ref,
                   *, nc, kc, tq, lambda_init):
    s_ref = (s0_ref, s1_ref, s2_ref, s3_ref)
    rows = 2 * tq
    lamv = lam_ref[...]
    lam = (jnp.exp(jnp.sum(lamv[0:1] * lamv[1:2], axis=-1, keepdims=True))
           - jnp.exp(jnp.sum(lamv[2:3] * lamv[3:4], axis=-1, keepdims=True)) + lambda_init)
    lane = lax.broadcasted_iota(jnp.int32, (tq, LANES), 1)
    low = lane < HEAD_DIM
    _init_pipeline(zeros=(s_ref[3], mb_ref), ones=(acc2_ref,))

    def stack(u):
        qh = q_ref[:, (u // 2) * LANES:(u // 2 + 1) * LANES]
        zero = jnp.zeros_like(qh)
        base = (u % 2) * HEAD_DIM
        in1 = (lane >= base) & (lane < base + C_SUB)
        in2 = (lane >= base + C_SUB) & (lane < base + HEAD_DIM)
        return jnp.concatenate([jnp.where(in1, qh, zero), jnp.where(in2, qh, zero)], axis=0)

    def qk(u):
        return (stack(u), kt_ref, (u // 2) * LANES, s_ref[u])

    def pv(u, mb):
        return (s_ref[u], mb, v_ref, u * LANES)

    def differential(acc):
        o = _normalise(acc)
        return o[0:tq] - lam * o[tq:2 * tq]

    def sub_norm(d_even, d_odd, g_ref):
        oi = jnp.where(low, d_even, pltpu.roll(d_odd, HEAD_DIM, 1))
        ms = _group_mean_sq(oi, g64_ref[...], HEAD_DIM)
        oc = oi * lax.rsqrt(ms + EPS) * sub_ref[...] * (1.0 - lambda_init)
        return (_silu(g_ref[...]) * oc).astype(BF16)

    mb0, acc3 = _stage(nc, kc, rows, qk=qk(0), pv=(s_ref[3], mb_ref[...], vprev_ref, 0))

    def finish_previous():
        o1_ref[...] = sub_norm(differential(acc2_ref[...]), differential(acc3), g1_ref)

    mb1, acc0 = _stage(nc, kc, rows, qk=qk(1), pv=pv(0, mb0), mid=finish_previous)
    mb2, acc1 = _stage(nc, kc, rows, qk=qk(2), pv=pv(1, mb1))

    def finish_current():
        o0_ref[...] = sub_norm(differential(acc0), differential(acc1), g0_ref)

    mb3, acc2 = _stage(nc, kc, rows, qk=qk(3), pv=pv(2, mb2), mid=finish_current)
    mb_ref[...] = mb3
    acc2_ref[...] = acc2


def _attn_c(qc, kct, vc, gc, lamv, c_subln, *, seq, tq, lambda_init):
    rows = qc.shape[0]
    _, nc, _, kc = kct.shape
    n_tiles = rows // tq
    full = lambda r, c: pl.BlockSpec((r, c), lambda i: (0, 0))
    in_specs, tile_cur, tile_prev = _pipeline_specs(
        tq, seq, n_tiles, GROUP_W, GROUP_W, nc, kc, 2 * GROUP_W, 3)
    return pl.pallas_call(
        functools.partial(_attn_c_kernel, nc=nc, kc=kc, tq=tq, lambda_init=lambda_init),
        grid=(n_tiles + 1,),
        in_specs=in_specs + [tile_cur(0), tile_prev(1),
                             full(4, C_SUB), full(1, LANES), full(LANES, LANES)],
        out_specs=[tile_cur(0), tile_prev(0)],
        out_shape=[jax.ShapeDtypeStruct((rows, LANES), BF16)] * 2,
        scratch_shapes=[pltpu.VMEM((nc, 2 * tq, kc), F32)] * (GROUP_W // HEAD_DIM)
        + [pltpu.VMEM((2 * tq, LANES), F32)] * 2,
        compiler_params=_params("arbitrary"),
        name="attn_c",
    )(qc, kct, vc, vc, gc, gc, lamv, jnp.tile(c_subln, LANES // HEAD_DIM)[None, :], _group_matrix(HEAD_DIM))


HY_CH = 4


def _hyconv_kernel(z_ref, kk_ref, nrm_ref, o_ref, acc_ref, *, nb, bsz):
    for ch in range(z_ref.shape[0]):
        _hyconv_channel(z_ref.at[ch], kk_ref.at[ch], nrm_ref.at[ch], o_ref.at[ch], acc_ref.at[ch],
                        nb, bsz)


def _hyconv_channel(z_ref, kk_ref, nrm_ref, o_ref, acc_ref, nb, bsz):
    z = z_ref[...].reshape(nb * bsz, LANES)
    kk = kk_ref[...] * (1.0 / nrm_ref[...])
    offset = (lax.broadcasted_iota(jnp.int32, (LANES, LANES), 1)
              - lax.broadcasted_iota(jnp.int32, (LANES, LANES), 0)).astype(F32).astype(BF16)
    rolled = {}

    def circulant(m):
        if m not in rolled:
            rolled[m] = pltpu.roll(jnp.broadcast_to(kk[m:m + 1], (LANES, LANES)), 0, 1,
                                   stride=1, stride_axis=0).astype(BF16)
        return rolled[m]

    def toeplitz(d):
        return jnp.where(offset >= 0, circulant(d + nb), circulant(d + nb - 1))

    acc_ref[...] = jnp.zeros_like(acc_ref)
    for d in range(0, nb, 2):
        m = bsz * (nb - d)
        out = _dot(z[0:m].astype(BF16), jnp.concatenate([toeplitz(d), toeplitz(d + 1)], axis=1))
        acc_ref[bsz * d:bsz * d + m, :] += out[:, 0:LANES]
        acc_ref[bsz * (d + 1):bsz * d + m, :] += out[0:m - bsz, LANES:2 * LANES]
    for e in range(2, nb, 2):
        m = bsz * (nb - e)
        out = _dot(z[bsz * e:bsz * e + m].astype(BF16),
                   jnp.concatenate([toeplitz(-e), toeplitz(-e - 1)], axis=1))
        acc_ref[0:m, :] += out[:, 0:LANES]
        acc_ref[0:m - bsz, :] += out[bsz:m, LANES:2 * LANES]
    m = bsz * (nb - 1)
    acc_ref[0:m, :] += _dot(z[bsz:bsz + m].astype(BF16), toeplitz(-1))
    o_ref[...] = acc_ref[...].reshape(nb, bsz, LANES)


def _hyconv(zt, kk3, nrm3):
    ch, nb, bsz, _ = zt.shape
    assert nb % 2 == 0 and ch % HY_CH == 0
    return pl.pallas_call(
        functools.partial(_hyconv_kernel, nb=nb, bsz=bsz),
        grid=(ch // HY_CH,),
        in_specs=[
            pl.BlockSpec((HY_CH, nb, bsz, LANES), lambda c: (c, 0, 0, 0)),
            pl.BlockSpec((HY_CH, 2 * nb, LANES), lambda c: (c, 0, 0)),
            pl.BlockSpec((HY_CH, 1, LANES), lambda c: (c, 0, 0)),
        ],
        out_specs=pl.BlockSpec((HY_CH, nb, bsz, LANES), lambda c: (c, 0, 0, 0)),
        out_shape=jax.ShapeDtypeStruct(zt.shape, F32),
        scratch_shapes=[pltpu.VMEM((HY_CH, nb * bsz, LANES), F32)],
        compiler_params=_params("arbitrary"),
        name="hyconv",
    )(zt, kk3, nrm3)


def _outproj_kernel(x_ref, mod_ref, ya0_ref, ya1_ref, gx0_ref, yb_ref, z_ref, hb_ref,
                    yc0_ref, yc1_ref, yd_ref, w_ref, o_ref, *, d_model):
    gate = mod_ref[0][:, 2 * d_model:3 * d_model]
    y_b = (gx0_ref[...] * (yb_ref[...] + hb_ref[...] * z_ref[...])).astype(BF16)
    cat = jnp.concatenate(
        [ya0_ref[...], ya1_ref[...], y_b, yc0_ref[...], yc1_ref[...], yd_ref[...]], axis=1)
    o_ref[...] = x_ref[...] + gate * _dot(cat, w_ref[...])


def _outproj(x2, mod3, ya, gx0, yb, z, hy_bias, yc, yd, w_out_bf, *, seq, tm):
    rows, d_model = x2.shape
    tps = seq // tm
    gw = GROUP_W
    rowblk = lambda c: pl.BlockSpec((tm, c), lambda i: (i, 0))
    full = lambda r, c: pl.BlockSpec((r, c), lambda i: (0, 0))
    g = rowblk(gw)
    hf = rowblk(LANES)
    return pl.pallas_call(
        functools.partial(_outproj_kernel, d_model=d_model),
        grid=(rows // tm,),
        in_specs=[
            rowblk(d_model),
            pl.BlockSpec((1, 1, 3 * d_model), lambda i: (i // tps, 0, 0)),
            hf, hf, g, g, g, full(1, gw), hf, hf, g,
            full(w_out_bf.shape[0], d_model),
        ],
        out_specs=rowblk(d_model),
        out_shape=jax.ShapeDtypeStruct((rows, d_model), F32),
        compiler_params=_params("arbitrary"),
        name="outproj",
    )(x2, mod3, *ya, gx0, yb, z, hy_bias[None, :], *yc, yd, w_out_bf)


def _rope_tables(seq):
    def cos_sin(pos, dim):
        inv = ROPE_THETA ** (-jnp.arange(0, dim, 2, dtype=F32) / dim)
        ang = pos.astype(F32)[:, None] * inv[None, :]
        return jnp.cos(ang), jnp.sin(ang)

    t = jnp.arange(seq, dtype=jnp.int32)
    cr, sr = cos_sin(t // GRID_W, HEAD_DIM // 2)
    cc, sc = cos_sin(t % GRID_W, HEAD_DIM // 2)
    cq, sq = cos_sin(t, C_SUB)
    cosa = jnp.tile(jnp.concatenate([cr, cr, cc, cc], axis=1), (1, LANES // HEAD_DIM))
    sina = jnp.tile(jnp.concatenate([-sr, sr, -sc, sc], axis=1), (1, LANES // HEAD_DIM))
    cosc = jnp.tile(jnp.concatenate([cq, cq], axis=1), (1, LANES // C_SUB))
    sinc = jnp.tile(jnp.concatenate([-sq, sq], axis=1), (1, LANES // C_SUB))
    return cosa, sina, cosc, sinc


def _tile_rows(seq, want):
    return want if seq % want == 0 else seq


def kernel(x, c, norm_g, w_ada, b_ada, w_in, w_out, a_qn, a_kn, hy_conv_w, hy_conv_b, hy_w1, hy_b1,
           hy_freq, hy_w2, hy_b2, hy_w3, hy_bias, c_qn, c_kn, lam_q1, lam_k1, lam_q2, lam_k2,
           c_subln, sc_conv_w):
    bsz, seq, d_model = x.shape
    depth = w_in.shape[0]
    nb = seq // LANES
    tm = _tile_rows(seq, 512)
    tq = _tile_rows(seq, 256)

    tables = _rope_tables(seq)
    mod = _ada(c, w_ada, b_ada)
    kk, nrm = _hyfilter(seq, hy_w1, hy_b1, hy_freq, hy_w2, hy_b2, hy_w3)
    w_in_bf = w_in.astype(BF16)
    w_out_bf = w_out.astype(BF16)

    x2 = x.reshape(bsz * seq, d_model)
    for l in range(depth):
        mod3 = mod[l][:, None, :]
        (qa, kat, va, ga, gx0, z, qc, kct, vc, gc, yd) = _inproj(
            x2, mod3, norm_g[l], w_in_bf[l], a_qn[l], a_kn[l], c_qn[l], c_kn[l], tables,
            hy_conv_w[l], hy_conv_b[l], sc_conv_w[l], seq=seq, tm=tm)
        ya = _attn_a(qa, kat, va, ga, seq=seq, tq=tq)
        lambda_init = 0.8 - 0.6 * math.exp(-0.3 * l)
        lamv = jnp.stack([lam_q1[l], lam_k1[l], lam_q2[l], lam_k2[l]])
        yc = _attn_c(qc, kct, vc, gc, lamv, c_subln[l], seq=seq, tq=tq, lambda_init=lambda_init)
        zt = z.reshape(bsz, nb, LANES, GROUP_W).transpose(3, 1, 0, 2)
        yt = _hyconv(zt, kk[l].reshape(GROUP_W, 2 * nb, LANES), nrm[l][:, None, :])
        yb = yt.transpose(2, 1, 3, 0).reshape(bsz * seq, GROUP_W)
        x2 = _outproj(x2, mod3, ya, gx0, yb, z, hy_bias[l], yc, yd, w_out_bf[l], seq=seq, tm=tm)
    return x2.reshape(bsz, seq, d_model)
```

```python
import functools
import math

import jax
import jax.numpy as jnp
from jax import lax
from jax.experimental import pallas as pl
from jax.experimental.pallas import tpu as pltpu

F32 = jnp.float32
BF16 = jnp.bfloat16

GROUP_W = 256
HEAD_DIM = 64
C_SUB = 32
HY_EMB = 33
HY_BANDS = (HY_EMB - 1) // 2
HY_SHIFT = 0.05
HY_FAST = 0.3
HY_SLOW = 1.5
HY_TARGET = 1e-2
GRID_W = 64
ROPE_THETA = 10000.0
EPS = 1e-6
LOG2E = 1.4426950408889634

LANES = 128
SUBLANES = 8
VMEM_LIMIT = 56 * 1024 * 1024
NEG_BIG = -3.0e38

HIGHEST = lax.Precision.HIGHEST


def _params(*sem):
    return pltpu.CompilerParams(dimension_semantics=sem, vmem_limit_bytes=VMEM_LIMIT)


def _dot(a, b):
    return jnp.dot(a, b, preferred_element_type=F32)


def _dot_exact(a, b):
    return jnp.dot(a, b, preferred_element_type=F32, precision=HIGHEST)


def _silu(x):
    return x * (1.0 / (1.0 + jnp.exp(-x)))


def _ada_kernel(c_ref, w_ref, b_ref, o_ref):
    o_ref[0] = _dot_exact(_silu(c_ref[...]), w_ref[0]) + b_ref[0]


def _ada(c, w_ada, b_ada):
    depth, d, d3 = w_ada.shape
    bsz = c.shape[0]
    nb = d3 // d
    return pl.pallas_call(
        _ada_kernel,
        grid=(depth, nb),
        in_specs=[
            pl.BlockSpec((bsz, d), lambda l, j: (0, 0)),
            pl.BlockSpec((1, d, d), lambda l, j: (l, 0, j)),
            pl.BlockSpec((1, 1, d), lambda l, j: (l, 0, j)),
        ],
        out_specs=pl.BlockSpec((1, bsz, d), lambda l, j: (l, 0, j)),
        out_shape=jax.ShapeDtypeStruct((depth, bsz, d3), F32),
        compiler_params=_params("arbitrary", "arbitrary"),
        name="ada",
    )(c, w_ada, b_ada.reshape(depth, 1, d3))


def _hyfilter_kernel(emb_ref, t_ref, w1_ref, b1_ref, fr_ref, w2_ref, b2_ref, w3_ref, dl_ref,
                     kk_ref, nrm_ref, *, seq, cb):
    j = pl.program_id(1)
    fr = fr_ref[0]
    h = jnp.sin(fr * (_dot_exact(w1_ref[0], emb_ref[...]) + b1_ref[0]))
    h = jnp.sin(fr * (_dot_exact(w2_ref[0], h) + b2_ref[0]))
    h = _dot_exact(w3_ref[0], h)
    window = jnp.exp(-t_ref[...] * dl_ref[...]) + HY_SHIFT
    col = j * cb + lax.broadcasted_iota(jnp.int32, (GROUP_W, cb), 1)
    kk = jnp.where(col >= seq, h[:GROUP_W], h[GROUP_W:]) * window
    kk = jnp.where(col == 0, 0.0, kk)
    kk_ref[0] = kk

    part = jnp.abs(kk[:, 0:LANES])
    for i in range(1, cb // LANES):
        part = part + jnp.abs(kk[:, i * LANES:(i + 1) * LANES])

    @pl.when(j == 0)
    def _():
        nrm_ref[0] = part

    @pl.when(j > 0)
    def _():
        nrm_ref[0] = nrm_ref[0] + part

    @pl.when(j == pl.num_programs(1) - 1)
    def _():
        tot = jnp.sum(nrm_ref[0], axis=-1, keepdims=True)
        nrm_ref[0] = jnp.broadcast_to(tot, (GROUP_W, LANES))


def _hyfilter(seq, hy_w1, hy_b1, hy_freq, hy_w2, hy_b2, hy_w3):
    depth = hy_w1.shape[0]
    ffn = hy_w1.shape[2]
    t = jnp.linspace(0.0, 1.0, seq, dtype=F32)[:, None]
    w = 2.0 * math.pi * jnp.arange(seq, dtype=F32)[:, None] / seq
    f = jnp.linspace(1e-4, HY_BANDS - 1, HY_BANDS, dtype=F32)[None, :]
    emb = jnp.concatenate([t, jnp.cos(f * w), -jnp.sin(f * w)], axis=-1)
    both = lambda v: jnp.concatenate([v[0:1], v[:0:-1], v], axis=0)
    emb2 = jnp.pad(both(emb).T, ((0, LANES - HY_EMB), (0, 0)))
    t2 = both(t).T
    max_decay = math.log(HY_TARGET) / HY_FAST
    min_decay = math.log(HY_TARGET) / HY_SLOW
    deltas = jnp.abs(jnp.linspace(min_decay, max_decay, GROUP_W, dtype=F32))[:, None]

    w1t = jnp.pad(jnp.swapaxes(hy_w1, 1, 2), ((0, 0), (0, 0), (0, LANES - HY_EMB)))
    w2t = jnp.swapaxes(hy_w2, 1, 2)
    w3t = jnp.swapaxes(hy_w3, 1, 2)
    col = lambda v: v[:, :, None]
    cb = min(1024, 2 * seq)
    nblk = (2 * seq) // cb
    wspec = lambda r, c: pl.BlockSpec((1, r, c), lambda l, j: (l, 0, 0))
    kk, nrm = pl.pallas_call(
        functools.partial(_hyfilter_kernel, seq=seq, cb=cb),
        grid=(depth, nblk),
        in_specs=[
            pl.BlockSpec((LANES, cb), lambda l, j: (0, j)),
            pl.BlockSpec((1, cb), lambda l, j: (0, j)),
            wspec(ffn, LANES), wspec(ffn, 1), wspec(ffn, 1), wspec(ffn, ffn), wspec(ffn, 1),
            wspec(2 * GROUP_W, ffn),
            pl.BlockSpec((GROUP_W, 1), lambda l, j: (0, 0)),
        ],
        out_specs=[
            pl.BlockSpec((1, GROUP_W, cb), lambda l, j: (l, 0, j)),
            pl.BlockSpec((1, GROUP_W, LANES), lambda l, j: (l, 0, 0)),
        ],
        out_shape=[
            jax.ShapeDtypeStruct((depth, GROUP_W, 2 * seq), F32),
            jax.ShapeDtypeStruct((depth, GROUP_W, LANES), F32),
        ],
        compiler_params=_params("arbitrary", "arbitrary"),
        name="hyfilter",
    )(emb2, t2, w1t, col(hy_b1), col(hy_freq), w2t, col(hy_b2), w3t, deltas)
    return kk, nrm


def _swap16(y):
    lane = lax.broadcasted_iota(jnp.int32, y.shape, 1)
    first = (lane & 31) < 16
    return jnp.where(first, pltpu.roll(y, LANES - 16, 1), pltpu.roll(y, 16, 1))


def _group_mean_sq(p, gmat, gsize):
    return _dot((p * p).astype(BF16), gmat) * (1.0 / gsize)


def _norm_rope(p, gain, gmat, gsize, cos, sin):
    ms = _group_mean_sq(p, gmat, gsize)
    y = p * lax.rsqrt(ms + EPS) * gain
    return y * cos + _swap16(y) * sin


def _store_values_with_ones(v_ref, slab, v):
    low = lax.broadcasted_iota(jnp.int32, v.shape, 1) < HEAD_DIM
    v_ref[:, slab * LANES:(slab + 1) * LANES] = jnp.where(low, v, 1.0).astype(BF16)
    v_ref[:, (slab + 1) * LANES:(slab + 2) * LANES] = jnp.where(
        low, pltpu.roll(v, HEAD_DIM, 1), 1.0).astype(BF16)


def _conv3(main, prev_row, next_row, w):
    tm = main.shape[0]
    rid = lax.broadcasted_iota(jnp.int32, main.shape, 0)
    up = jnp.where(rid == 0, prev_row, pltpu.roll(main, 1, 0))
    dn = jnp.where(rid == tm - 1, next_row, pltpu.roll(main, tm - 1, 0))
    return up * w[0:1] + main * w[1:2] + dn * w[2:3]


def _inproj_kernel(x_ref, xprev_ref, xnext_ref, mod_ref, ng_ref, w_ref, aqn_ref, akn_ref, cqn_ref,
                   ckn_ref, g64_ref, g32_ref, cosa_ref, sina_ref, cosc_ref, sinc_ref,
                   hyw_ref, hyb_ref, scw_ref,
                   qa_ref, kat_ref, va_ref, ga_ref, gx0_ref, z_ref,
                   qc_ref, kct_ref, vc_ref, gc_ref, yd_ref, *, d_model, tps):
    tm = x_ref.shape[0]
    i = pl.program_id(0)
    has_prev = (i % tps != 0).astype(F32)
    has_next = (i % tps != tps - 1).astype(F32)
    x = jnp.concatenate([x_ref[...], xprev_ref[...], xnext_ref[...]], axis=0)
    prev_at, next_at = tm + SUBLANES - 1, tm + SUBLANES
    mod = mod_ref[0]
    shift = mod[:, 0:d_model]
    scale = mod[:, d_model:2 * d_model]
    ms = jnp.mean(x * x, axis=-1, keepdims=True)
    h = x * lax.rsqrt(ms + EPS) * ng_ref[...]
    hb_ext = (h * (1.0 + scale) + shift).astype(BF16)
    hb = hb_ext[0:tm]

    def proj(lo, width):
        return _dot(hb, w_ref[:, lo:lo + width])

    def proj_with_neighbours(lo, width):
        return _dot(hb_ext, w_ref[:, lo:lo + width])

    g64 = g64_ref[...]
    g32 = g32_ref[...]
    cosa, sina = cosa_ref[...], sina_ref[...]
    cosc, sinc = cosc_ref[...], sinc_ref[...]
    gw = GROUP_W
    sa = HEAD_DIM ** -0.5 * LOG2E
    sc = C_SUB ** -0.5 * LOG2E

    hd = HEAD_DIM

    def norm_rope_a(p, gain_ref):
        return _norm_rope(p, gain_ref[...], g64, HEAD_DIM, cosa, sina)

    def norm_rope_c(p, gain_ref):
        return _norm_rope(p, gain_ref[...], g32, C_SUB, cosc, sinc)

    def a_query(p):
        for i in range(gw // LANES):
            q = norm_rope_a(p[:, i * LANES:(i + 1) * LANES], aqn_ref)
            qa_ref[:, i * LANES:(i + 1) * LANES] = (q * sa).astype(BF16)

    def a_key_value(p):
        kt = norm_rope_a(p[:, 0:LANES], akn_ref).T.astype(BF16)
        kat_ref[0, 0, 0 * hd:1 * hd, :] = kt[0:hd]
        kat_ref[0, 0, 1 * hd:2 * hd, :] = kt[0:hd]
        kat_ref[0, 0, 2 * hd:3 * hd, :] = kt[hd:2 * hd]
        kat_ref[0, 0, 3 * hd:4 * hd, :] = kt[hd:2 * hd]
        _store_values_with_ones(va_ref, 0, p[:, LANES:2 * LANES])

    def a_gate(p):
        ga_ref[...] = p

    held = {}

    def b_conv(p):
        y = _conv3(p[0:tm], p[prev_at:prev_at + 1] * has_prev, p[next_at:next_at + 1] * has_next,
                   hyw_ref[...]) + hyb_ref[...]
        z_ref[...] = y[:, gw:2 * gw] * y[:, 2 * gw:3 * gw]
        held["x0"] = y[:, 0:gw]

    def b_gate(p):
        gx0_ref[...] = _silu(p) * held["x0"]

    def c_query(p):
        for i in range(gw // LANES):
            q = norm_rope_c(p[:, i * LANES:(i + 1) * LANES], cqn_ref)
            qc_ref[:, i * LANES:(i + 1) * LANES] = (q * sc).astype(BF16)

    def c_key(p):
        for i in range(gw // LANES):
            k = norm_rope_c(p[:, i * LANES:(i + 1) * LANES], ckn_ref)
            kct_ref[0, 0, i * LANES:(i + 1) * LANES, :] = k.T.astype(BF16)

    def c_value(p):
        for i in range(gw // LANES):
            _store_values_with_ones(vc_ref, 2 * i, p[:, i * LANES:(i + 1) * LANES])

    def c_gate(p):
        gc_ref[...] = p

    def d_conv(p):
        u = p[:, gw:2 * gw] * p[:, 2 * gw:3 * gw]
        cv = _conv3(u[0:tm], u[prev_at:prev_at + 1] * has_prev, u[next_at:next_at + 1] * has_next,
                    scw_ref[...])
        held["d"] = p[0:tm, 0:gw] * cv

    def d_gate(p):
        yd_ref[...] = (_silu(p) * held["d"]).astype(BF16)

    groups = [
        (gw, False, a_query), (gw, False, a_key_value), (gw, False, a_gate),
        (3 * gw, True, b_conv), (gw, False, b_gate),
        (gw, False, c_query), (gw, False, c_key), (gw, False, c_value), (gw, False, c_gate),
        (3 * gw, True, d_conv), (gw, False, d_gate),
    ]
    off = 0
    pending = None
    for width, ext, epilogue in groups:
        p = (proj_with_neighbours if ext else proj)(off, width)
        off += width
        if pending is not None:
            pending[0](pending[1])
        pending = (epilogue, p)
    pending[0](pending[1])


def _group_matrix(gsize):
    i = jnp.arange(LANES) // gsize
    return (i[:, None] == i[None, :]).astype(BF16)


def _halo_specs(tm, width, nrows):
    per = tm // SUBLANES
    last = nrows // SUBLANES - 1
    prev = pl.BlockSpec((SUBLANES, width), lambda i: (jnp.maximum(i * per - 1, 0), 0))
    nxt = pl.BlockSpec((SUBLANES, width), lambda i: (jnp.minimum((i + 1) * per, last), 0))
    return prev, nxt


def _inproj(x2, mod3, norm_g, w_in_bf, a_qn, a_kn, c_qn, c_kn, tables, hy_conv_w, hy_conv_b,
            sc_conv_w, *, seq, tm, layer):
    rows, d_model = x2.shape
    d_in = w_in_bf.shape[2]
    tps = seq // tm
    bsz = rows // seq
    gw = GROUP_W
    rep = lambda v, g: jnp.tile(v, LANES // g)[None, :]
    full = lambda r, c: pl.BlockSpec((r, c), lambda i: (0, 0))
    rowblk = lambda c: pl.BlockSpec((tm, c), lambda i: (i, 0))
    tab = pl.BlockSpec((tm, LANES), lambda i: (i % tps, 0))
    ktspec = lambda r: pl.BlockSpec((1, 1, r, tm), lambda i: (i // tps, i % tps, 0, 0))
    out_shape = [
        jax.ShapeDtypeStruct((rows, gw), BF16),
        jax.ShapeDtypeStruct((bsz, tps, 2 * LANES, tm), BF16),
        jax.ShapeDtypeStruct((rows, 2 * LANES), BF16),
        jax.ShapeDtypeStruct((rows, gw), F32),
        jax.ShapeDtypeStruct((rows, gw), F32),
        jax.ShapeDtypeStruct((rows, gw), F32),
        jax.ShapeDtypeStruct((rows, gw), BF16),
        jax.ShapeDtypeStruct((bsz, tps, gw, tm), BF16),
        jax.ShapeDtypeStruct((rows, 2 * gw), BF16),
        jax.ShapeDtypeStruct((rows, gw), F32),
        jax.ShapeDtypeStruct((rows, gw), BF16),
    ]
    out_specs = [
        rowblk(gw), ktspec(2 * LANES), rowblk(2 * LANES), rowblk(gw), rowblk(gw), rowblk(gw),
        rowblk(gw), ktspec(gw), rowblk(2 * gw), rowblk(gw), rowblk(gw),
    ]
    xprev, xnext = _halo_specs(tm, d_model, rows)
    return pl.pallas_call(
        functools.partial(_inproj_kernel, d_model=d_model, tps=tps),
        grid=(rows // tm,),
        in_specs=[
            rowblk(d_model), xprev, xnext,
            pl.BlockSpec((1, 1, 3 * d_model), lambda i: (i // tps, 0, 0)),
            full(1, d_model),
            pl.BlockSpec((None, d_model, d_in), lambda i: (layer, 0, 0)),
            full(1, LANES), full(1, LANES), full(1, LANES), full(1, LANES),
            full(LANES, LANES), full(LANES, LANES),
            tab, tab, tab, tab,
            full(3, 3 * gw), full(1, 3 * gw), full(3, gw),
        ],
        out_specs=out_specs,
        out_shape=out_shape,
        compiler_params=_params("arbitrary"),
        name="inproj",
    )(x2, x2, x2, mod3, norm_g[None, :], w_in_bf, rep(a_qn, HEAD_DIM), rep(a_kn, HEAD_DIM),
      rep(c_qn, C_SUB), rep(c_kn, C_SUB), _group_matrix(HEAD_DIM), _group_matrix(C_SUB), *tables,
      hy_conv_w, hy_conv_b[None, :], sc_conv_w)


def _stage(nc, kc, rows, qk=None, pv=None, mid=None):
    mrun = jnp.full((rows, LANES), NEG_BIG, F32)
    acc = jnp.zeros((rows, LANES), F32)
    for c in range(nc):
        if qk is not None:
            q2, kt_ref, krow, s_out = qk
            s = _dot(q2, kt_ref[0, c, krow:krow + LANES, :])
            s_out[c] = s
            for j in range(kc // LANES):
                mrun = jnp.maximum(mrun, s[:, j * LANES:(j + 1) * LANES])
        if pv is not None:
            s_in, mb, v_ref, vcol = pv
            s = s_in[c]
            pb = jnp.concatenate(
                [jnp.exp2(s[:, j * LANES:(j + 1) * LANES] - mb).astype(BF16)
                 for j in range(kc // LANES)], axis=1)
            acc = acc + _dot(pb, v_ref[c * kc:(c + 1) * kc, vcol:vcol + LANES])
        if mid is not None and c == nc // 2:
            mid()
    mb_new =jnp.broadcast_to(jnp.max(mrun, axis=-1, keepdims=True), (rows, LANES))
    return mb_new, acc


def _normalise(acc):
    return acc * (1.0 / pltpu.roll(acc, HEAD_DIM, 1))


def _init_pipeline(zeros=(), ones=()):
    @pl.when(pl.program_id(0) == 0)
    def _():
        for r in zeros:
            r[...] = jnp.zeros(r.shape, r.dtype)
        for r in ones:
            r[...] = jnp.ones(r.shape, r.dtype)


def _attn_a_kernel(q_ref, kt_ref, v_ref, vprev_ref, g0_ref, g1_ref, o0_ref, o1_ref,
                   s0_ref, s1_ref, mb_ref, acc_ref, *, nc, kc, tq):
    s_ref = (s0_ref, s1_ref)
    rows = 2 * tq
    low = lax.broadcasted_iota(jnp.int32, (tq, LANES), 1) < HEAD_DIM
    _init_pipeline(zeros=(s_ref[1], mb_ref), ones=(acc_ref,))

    def stack(h):
        qh = q_ref[:, h * LANES:(h + 1) * LANES]
        zero = jnp.zeros_like(qh)
        return jnp.concatenate([jnp.where(low, qh, zero), jnp.where(low, zero, qh)], axis=0)

    def finish(acc, g_ref):
        o = _normalise(acc)
        o = jnp.where(low, o[0:tq], pltpu.roll(o[tq:2 * tq], HEAD_DIM, 1))
        return (_silu(g_ref[...]) * o).astype(BF16)

    def finish_previous_0():
        o0_ref[...] = finish(acc_ref[...], g0_ref)

    mb0, acc1 = _stage(nc, kc, rows, qk=(stack(0), kt_ref, 0, s_ref[0]),
                       pv=(s_ref[1], mb_ref[...], vprev_ref, 0), mid=finish_previous_0)

    def finish_previous_1():
        o1_ref[...] = finish(acc1, g1_ref)

    mb1, acc0 = _stage(nc, kc, rows, qk=(stack(1), kt_ref, LANES, s_ref[1]),
                       pv=(s_ref[0], mb0, v_ref, 0), mid=finish_previous_1)
    mb_ref[...] = mb1
    acc_ref[...] = acc0


def _pipeline_specs(tq, seq, n_tiles, q_width, kt_rows, nc, kc, v_width, last_unit):
    nq = seq // tq
    cur = lambda i: jnp.minimum(i, n_tiles - 1)
    prev = lambda i: jnp.maximum(i - 1, 0)
    in_specs = [
        pl.BlockSpec((tq, q_width), lambda i: (cur(i), 0)),
        pl.BlockSpec((1, nc, kt_rows, kc), lambda i: (cur(i) // nq, 0, 0, 0)),
        pl.BlockSpec((seq, v_width), lambda i: (cur(i) // nq, 0)),
        pl.BlockSpec((seq, LANES), lambda i: (prev(i) // nq, last_unit)),
    ]
    tile_cur = lambda col: pl.BlockSpec((tq, LANES), lambda i: (cur(i), col))
    tile_prev = lambda col: pl.BlockSpec((tq, LANES), lambda i: (prev(i), col))
    return in_specs, tile_cur, tile_prev


def _attn_a(qa, kat, va, ga, *, seq, tq):
    rows = qa.shape[0]
    _, nc, _, kc = kat.shape
    n_tiles = rows // tq
    in_specs, _, tile_prev = _pipeline_specs(tq, seq, n_tiles, GROUP_W, 2 * LANES, nc, kc, 2 * LANES, 1)
    return pl.pallas_call(
        functools.partial(_attn_a_kernel, nc=nc, kc=kc, tq=tq),
        grid=(n_tiles + 1,),
        in_specs=in_specs + [tile_prev(0), tile_prev(1)],
        out_specs=[tile_prev(0), tile_prev(0)],
        out_shape=[jax.ShapeDtypeStruct((rows, LANES), BF16)] * 2,
        scratch_shapes=[pltpu.VMEM((nc, 2 * tq, kc), F32)] * 2 + [pltpu.VMEM((2 * tq, LANES), F32)] * 2,
        compiler_params=_params("arbitrary"),
        name="attn_a",
    )(qa, kat, va, va, ga, ga)


def _attn_c_kernel(q_ref, kt_ref, v_ref, vprev_ref, g0_ref, g1_ref, lam_ref, sub_ref, g64_ref,
                   o0_ref, o1_ref, s0_ref, s1_ref, s2_ref, s3_ref, mb_ref, acc2_ref,
                   *, nc, kc, tq, lambda_init):
    s_ref = (s0_ref, s1_ref, s2_ref, s3_ref)
    rows = 2 * tq
    lamv = lam_ref[...]
    lam = (jnp.exp(jnp.sum(lamv[0:1] * lamv[1:2], axis=-1, keepdims=True))
           - jnp.exp(jnp.sum(lamv[2:3] * lamv[3:4], axis=-1, keepdims=True)) + lambda_init)
    lane = lax.broadcasted_iota(jnp.int32, (tq, LANES), 1)
    low = lane < HEAD_DIM
    _init_pipeline(zeros=(s_ref[3], mb_ref), ones=(acc2_ref,))

    def stack(u):
        qh = q_ref[:, (u // 2) * LANES:(u // 2 + 1) * LANES]
        zero = jnp.zeros_like(qh)
        base = (u % 2) * HEAD_DIM
        in1 = (lane >= base) & (lane < base + C_SUB)
        in2 = (lane >= base + C_SUB) & (lane < base + HEAD_DIM)
        return jnp.concatenate([jnp.where(in1, qh, zero), jnp.where(in2, qh, zero)], axis=0)

    def qk(u):
        return (stack(u), kt_ref, (u // 2) * LANES, s_ref[u])

    def pv(u, mb):
        return (s_ref[u], mb, v_ref, u * LANES)

    def differential(acc):
        o = _normalise(acc)
        return o[0:tq] - lam * o[tq:2 * tq]

    def sub_norm(d_even, d_odd, g_ref):
        oi = jnp.where(low, d_even, pltpu.roll(d_odd, HEAD_DIM, 1))
        ms = _group_mean_sq(oi, g64_ref[...], HEAD_DIM)
        oc = oi * lax.rsqrt(ms + EPS) * sub_ref[...] * (1.0 - lambda_init)
        return (_silu(g_ref[...]) * oc).astype(BF16)

    mb0, acc3 = _stage(nc, kc, rows, qk=qk(0), pv=(s_ref[3], mb_ref[...], vprev_ref, 0))

    def finish_previous():
        o1_ref[...] = sub_norm(differential(acc2_ref[...]), differential(acc3), g1_ref)

    mb1, acc0 = _stage(nc, kc, rows, qk=qk(1), pv=pv(0, mb0), mid=finish_previous)
    mb2, acc1 = _stage(nc, kc, rows, qk=qk(2), pv=pv(1, mb1))

    def finish_current():
        o0_ref[...] = sub_norm(differential(acc0), differential(acc1), g0_ref)

    mb3, acc2 = _stage(nc, kc, rows, qk=qk(3), pv=pv(2, mb2), mid=finish_current)
    mb_ref[...] = mb3
    acc2_ref[...] = acc2


def _attn_c(qc, kct, vc, gc, lamv, c_subln, *, seq, tq, lambda_init):
    rows = qc.shape[0]
    _, nc, _, kc = kct.shape
    n_tiles = rows // tq
    full = lambda r, c: pl.BlockSpec((r, c), lambda i: (0, 0))
    in_specs, tile_cur, tile_prev = _pipeline_specs(
        tq, seq, n_tiles, GROUP_W, GROUP_W, nc, kc, 2 * GROUP_W, 3)
    return pl.pallas_call(
        functools.partial(_attn_c_kernel, nc=nc, kc=kc, tq=tq, lambda_init=lambda_init),
        grid=(n_tiles + 1,),
        in_specs=in_specs + [tile_cur(0), tile_prev(1),
                             full(4, C_SUB), full(1, LANES), full(LANES, LANES)],
        out_specs=[tile_cur(0), tile_prev(0)],
        out_shape=[jax.ShapeDtypeStruct((rows, LANES), BF16)] * 2,
        scratch_shapes=[pltpu.VMEM((nc, 2 * tq, kc), F32)] * (GROUP_W // HEAD_DIM)
        + [pltpu.VMEM((2 * tq, LANES), F32)] * 2,
        compiler_params=_params("arbitrary"),
        name="attn_c",
    )(qc, kct, vc, vc, gc, gc, lamv, jnp.tile(c_subln, LANES // HEAD_DIM)[None, :], _group_matrix(HEAD_DIM))


HY_CH = 8


def _hyconv_kernel(z_ref, kk_ref, nrm_ref, o_ref, acc_ref, *, nb, bsz):
    for ch in range(z_ref.shape[0]):
        _hyconv_channel(z_ref.at[ch], kk_ref.at[ch], nrm_ref.at[ch], o_ref.at[ch], acc_ref.at[ch],
                        nb, bsz)


def _hyconv_channel(z_ref, kk_ref, nrm_ref, o_ref, acc_ref, nb, bsz):
    z = z_ref[...].reshape(nb * bsz, LANES)
    kk = kk_ref[...] * (1.0 / nrm_ref[...])
    offset = (lax.broadcasted_iota(jnp.int32, (LANES, LANES), 1)
              - lax.broadcasted_iota(jnp.int32, (LANES, LANES), 0)).astype(F32).astype(BF16)
    rolled = {}

    def circulant(m):
        if m not in rolled:
            rolled[m] = pltpu.roll(jnp.broadcast_to(kk[m:m + 1], (LANES, LANES)), 0, 1,
                                   stride=1, stride_axis=0).astype(BF16)
        return rolled[m]

    def toeplitz(d):
        return jnp.where(offset >= 0, circulant(d + nb), circulant(d + nb - 1))

    acc_ref[...] = jnp.zeros_like(acc_ref)
    for d in range(0, nb, 2):
        m = bsz * (nb - d)
        out = _dot(z[0:m].astype(BF16), jnp.concatenate([toeplitz(d), toeplitz(d + 1)], axis=1))
        acc_ref[bsz * d:bsz * d + m, :] += out[:, 0:LANES]
        acc_ref[bsz * (d + 1):bsz * d + m, :] += out[0:m - bsz, LANES:2 * LANES]
    for e in range(2, nb, 2):
        m = bsz * (nb - e)
        out = _dot(z[bsz * e:bsz * e + m].astype(BF16),
                   jnp.concatenate([toeplitz(-e), toeplitz(-e - 1)], axis=1))
        acc_ref[0:m, :] += out[:, 0:LANES]
        acc_ref[0:m - bsz, :] += out[bsz:m, LANES:2 * LANES]
    m = bsz * (nb - 1)
    acc_ref[0:m, :] += _dot(z[bsz:bsz + m].astype(BF16), toeplitz(-1))
    o_ref[...] = acc_ref[...].reshape(nb, bsz, LANES)


def _hyconv(zt, kk3, nrm3, layer):
    ch, nb, bsz, _ = zt.shape
    assert nb % 2 == 0 and ch % HY_CH == 0
    return pl.pallas_call(
        functools.partial(_hyconv_kernel, nb=nb, bsz=bsz),
        grid=(ch // HY_CH,),
        in_specs=[
            pl.BlockSpec((HY_CH, nb, bsz, LANES), lambda c: (c, 0, 0, 0)),
            pl.BlockSpec((None, HY_CH, 2 * nb, LANES), lambda c: (layer, c, 0, 0)),
            pl.BlockSpec((None, HY_CH, 1, LANES), lambda c: (layer, c, 0, 0)),
        ],
        out_specs=pl.BlockSpec((HY_CH, nb, bsz, LANES), lambda c: (c, 0, 0, 0)),
        out_shape=jax.ShapeDtypeStruct(zt.shape, F32),
        scratch_shapes=[pltpu.VMEM((HY_CH, nb * bsz, LANES), F32)],
        compiler_params=_params("arbitrary"),
        name="hyconv",
    )(zt, kk3, nrm3)


def _outproj_kernel(x_ref, mod_ref, ya0_ref, ya1_ref, gx0_ref, yb_ref, z_ref, hb_ref,
                    yc0_ref, yc1_ref, yd_ref, w_ref, o_ref, *, d_model):
    gate = mod_ref[0][:, 2 * d_model:3 * d_model]
    y_b = (gx0_ref[...] * (yb_ref[...] + hb_ref[...] * z_ref[...])).astype(BF16)
    cat = jnp.concatenate(
        [ya0_ref[...], ya1_ref[...], y_b, yc0_ref[...], yc1_ref[...], yd_ref[...]], axis=1)
    o_ref[...] = x_ref[...] + gate * _dot(cat, w_ref[...])


def _outproj(x2, mod3, ya, gx0, yb, z, hy_bias, yc, yd, w_out_bf, *, seq, tm, layer):
    rows, d_model = x2.shape
    tps = seq // tm
    gw = GROUP_W
    rowblk = lambda c: pl.BlockSpec((tm, c), lambda i: (i, 0))
    full = lambda r, c: pl.BlockSpec((r, c), lambda i: (0, 0))
    g = rowblk(gw)
    hf = rowblk(LANES)
    return pl.pallas_call(
        functools.partial(_outproj_kernel, d_model=d_model),
        grid=(rows // tm,),
        in_specs=[
            rowblk(d_model),
            pl.BlockSpec((1, 1, 3 * d_model), lambda i: (i // tps, 0, 0)),
            hf, hf, g, g, g, full(1, gw), hf, hf, g,
            pl.BlockSpec((None, w_out_bf.shape[1], d_model), lambda i: (layer, 0, 0)),
        ],
        out_specs=rowblk(d_model),
        out_shape=jax.ShapeDtypeStruct((rows, d_model), F32),
        compiler_params=_params("arbitrary"),
        name="outproj",
    )(x2, mod3, *ya, gx0, yb, z, hy_bias[None, :], *yc, yd, w_out_bf)


def _rope_tables(seq):
    def cos_sin(pos, dim):
        inv = ROPE_THETA ** (-jnp.arange(0, dim, 2, dtype=F32) / dim)
        ang = pos.astype(F32)[:, None] * inv[None, :]
        return jnp.cos(ang), jnp.sin(ang)

    t = jnp.arange(seq, dtype=jnp.int32)
    cr, sr = cos_sin(t // GRID_W, HEAD_DIM // 2)
    cc, sc = cos_sin(t % GRID_W, HEAD_DIM // 2)
    cq, sq = cos_sin(t, C_SUB)
    cosa = jnp.tile(jnp.concatenate([cr, cr, cc, cc], axis=1), (1, LANES // HEAD_DIM))
    sina = jnp.tile(jnp.concatenate([-sr, sr, -sc, sc], axis=1), (1, LANES // HEAD_DIM))
    cosc = jnp.tile(jnp.concatenate([cq, cq], axis=1), (1, LANES // C_SUB))
    sinc = jnp.tile(jnp.concatenate([-sq, sq], axis=1), (1, LANES // C_SUB))
    return cosa, sina, cosc, sinc


def _tile_rows(seq, want):
    return want if seq % want == 0 else seq


def kernel(x, c, norm_g, w_ada, b_ada, w_in, w_out, a_qn, a_kn, hy_conv_w, hy_conv_b, hy_w1, hy_b1,
           hy_freq, hy_w2, hy_b2, hy_w3, hy_bias, c_qn, c_kn, lam_q1, lam_k1, lam_q2, lam_k2,
           c_subln, sc_conv_w):
    bsz, seq, d_model = x.shape
    depth = w_in.shape[0]
    nb = seq // LANES
    tm = _tile_rows(seq, 512)
    tq = _tile_rows(seq, 256)
    tq_a = _tile_rows(seq, 512)

    tables = _rope_tables(seq)
    mod = _ada(c, w_ada, b_ada)
    kk, nrm = _hyfilter(seq, hy_w1, hy_b1, hy_freq, hy_w2, hy_b2, hy_w3)
    w_in_bf = w_in.astype(BF16)
    w_out_bf = w_out.astype(BF16)

    x2 = x.reshape(bsz * seq, d_model)
    for l in range(depth):
        mod3 = mod[l][:, None, :]
        (qa, kat, va, ga, gx0, z, qc, kct, vc, gc, yd) = _inproj(
            x2, mod3, norm_g[l], w_in_bf, a_qn[l], a_kn[l], c_qn[l], c_kn[l], tables,
            hy_conv_w[l], hy_conv_b[l], sc_conv_w[l], seq=seq, tm=tm, layer=l)
        ya = _attn_a(qa, kat, va, ga, seq=seq, tq=tq_a)
        lambda_init = 0.8 - 0.6 * math.exp(-0.3 * l)
        lamv = jnp.stack([lam_q1[l], lam_k1[l], lam_q2[l], lam_k2[l]])
        yc = _attn_c(qc, kct, vc, gc, lamv, c_subln[l], seq=seq, tq=tq, lambda_init=lambda_init)
        zt = z.reshape(bsz, nb, LANES, GROUP_W).transpose(3, 1, 0, 2)
        yt = _hyconv(zt, kk.reshape(depth, GROUP_W, 2 * nb, LANES), nrm[:, :, None, :], l)
        yb = yt.transpose(2, 1, 3, 0).reshape(bsz * seq, GROUP_W)
        x2 = _outproj(x2, mod3, ya, gx0, yb, z, hy_bias[l], yc, yd, w_out_bf, seq=seq, tm=tm,
                      layer=l)
    return x2.reshape(bsz, seq, d_model)
```

```python
import functools
import math

import jax
import jax.numpy as jnp
from jax import lax
from jax.experimental import pallas as pl
from jax.experimental.pallas import tpu as pltpu

F32 = jnp.float32
BF16 = jnp.bfloat16

GROUP_W = 256
HEAD_DIM = 64
C_SUB = 32
HY_EMB = 33
HY_BANDS = (HY_EMB - 1) // 2
HY_SHIFT = 0.05
HY_FAST = 0.3
HY_SLOW = 1.5
HY_TARGET = 1e-2
GRID_W = 64
ROPE_THETA = 10000.0
EPS = 1e-6
LOG2E = 1.4426950408889634

LANES = 128
SUBLANES = 8
VMEM_LIMIT = 56 * 1024 * 1024
NEG_BIG = -3.0e38

HIGHEST = lax.Precision.HIGHEST


def _params(*sem):
    return pltpu.CompilerParams(dimension_semantics=sem, vmem_limit_bytes=VMEM_LIMIT)


def _dot(a, b):
    return jnp.dot(a, b, preferred_element_type=F32)


def _dot_exact(a, b):
    return jnp.dot(a, b, preferred_element_type=F32, precision=HIGHEST)


def _silu(x):
    return x * (1.0 / (1.0 + jnp.exp(-x)))


def _ada_kernel(c_ref, w_ref, b_ref, o_ref):
    o_ref[0] = _dot_exact(_silu(c_ref[...]), w_ref[0]) + b_ref[0]


def _ada(c, w_ada, b_ada):
    depth, d, d3 = w_ada.shape
    bsz = c.shape[0]
    nb = d3 // d
    return pl.pallas_call(
        _ada_kernel,
        grid=(depth, nb),
        in_specs=[
            pl.BlockSpec((bsz, d), lambda l, j: (0, 0)),
            pl.BlockSpec((1, d, d), lambda l, j: (l, 0, j)),
            pl.BlockSpec((1, 1, d), lambda l, j: (l, 0, j)),
        ],
        out_specs=pl.BlockSpec((1, bsz, d), lambda l, j: (l, 0, j)),
        out_shape=jax.ShapeDtypeStruct((depth, bsz, d3), F32),
        compiler_params=_params("arbitrary", "arbitrary"),
        name="ada",
    )(c, w_ada, b_ada.reshape(depth, 1, d3))


def _hyfilter_kernel(emb_ref, t_ref, w1_ref, b1_ref, fr_ref, w2_ref, b2_ref, w3_ref, dl_ref,
                     kk_ref, nrm_ref, *, seq, cb):
    j = pl.program_id(1)
    fr = fr_ref[0]
    h = jnp.sin(fr * (_dot_exact(w1_ref[0], emb_ref[...]) + b1_ref[0]))
    h = jnp.sin(fr * (_dot_exact(w2_ref[0], h) + b2_ref[0]))
    h = _dot_exact(w3_ref[0], h)
    window = jnp.exp(-t_ref[...] * dl_ref[...]) + HY_SHIFT
    col = j * cb + lax.broadcasted_iota(jnp.int32, (GROUP_W, cb), 1)
    kk = jnp.where(col >= seq, h[:GROUP_W], h[GROUP_W:]) * window
    kk = jnp.where(col == 0, 0.0, kk)
    kk_ref[0] = kk

    part = jnp.abs(kk[:, 0:LANES])
    for i in range(1, cb // LANES):
        part = part + jnp.abs(kk[:, i * LANES:(i + 1) * LANES])

    @pl.when(j == 0)
    def _():
        nrm_ref[0] = part

    @pl.when(j > 0)
    def _():
        nrm_ref[0] = nrm_ref[0] + part

    @pl.when(j == pl.num_programs(1) - 1)
    def _():
        tot = jnp.sum(nrm_ref[0], axis=-1, keepdims=True)
        nrm_ref[0] = jnp.broadcast_to(tot, (GROUP_W, LANES))


def _hyfilter(seq, hy_w1, hy_b1, hy_freq, hy_w2, hy_b2, hy_w3):
    depth = hy_w1.shape[0]
    ffn = hy_w1.shape[2]
    t = jnp.linspace(0.0, 1.0, seq, dtype=F32)[:, None]
    w = 2.0 * math.pi * jnp.arange(seq, dtype=F32)[:, None] / seq
    f = jnp.linspace(1e-4, HY_BANDS - 1, HY_BANDS, dtype=F32)[None, :]
    emb = jnp.concatenate([t, jnp.cos(f * w), -jnp.sin(f * w)], axis=-1)
    both = lambda v: jnp.concatenate([v[0:1], v[:0:-1], v], axis=0)
    emb2 = jnp.pad(both(emb).T, ((0, LANES - HY_EMB), (0, 0)))
    t2 = both(t).T
    max_decay = math.log(HY_TARGET) / HY_FAST
    min_decay = math.log(HY_TARGET) / HY_SLOW
    deltas = jnp.abs(jnp.linspace(min_decay, max_decay, GROUP_W, dtype=F32))[:, None]

    w1t = jnp.pad(jnp.swapaxes(hy_w1, 1, 2), ((0, 0), (0, 0), (0, LANES - HY_EMB)))
    w2t = jnp.swapaxes(hy_w2, 1, 2)
    w3t = jnp.swapaxes(hy_w3, 1, 2)
    col = lambda v: v[:, :, None]
    cb = min(1024, 2 * seq)
    nblk = (2 * seq) // cb
    wspec = lambda r, c: pl.BlockSpec((1, r, c), lambda l, j: (l, 0, 0))
    kk, nrm = pl.pallas_call(
        functools.partial(_hyfilter_kernel, seq=seq, cb=cb),
        grid=(depth, nblk),
        in_specs=[
            pl.BlockSpec((LANES, cb), lambda l, j: (0, j)),
            pl.BlockSpec((1, cb), lambda l, j: (0, j)),
            wspec(ffn, LANES), wspec(ffn, 1), wspec(ffn, 1), wspec(ffn, ffn), wspec(ffn, 1),
            wspec(2 * GROUP_W, ffn),
            pl.BlockSpec((GROUP_W, 1), lambda l, j: (0, 0)),
        ],
        out_specs=[
            pl.BlockSpec((1, GROUP_W, cb), lambda l, j: (l, 0, j)),
            pl.BlockSpec((1, GROUP_W, LANES), lambda l, j: (l, 0, 0)),
        ],
        out_shape=[
            jax.ShapeDtypeStruct((depth, GROUP_W, 2 * seq), F32),
            jax.ShapeDtypeStruct((depth, GROUP_W, LANES), F32),
        ],
        compiler_params=_params("arbitrary", "arbitrary"),
        name="hyfilter",
    )(emb2, t2, w1t, col(hy_b1), col(hy_freq), w2t, col(hy_b2), w3t, deltas)
    return kk, nrm


def _swap16(y):
    lane = lax.broadcasted_iota(jnp.int32, y.shape, 1)
    first = (lane & 31) < 16
    return jnp.where(first, pltpu.roll(y, LANES - 16, 1), pltpu.roll(y, 16, 1))


def _group_mean_sq(p, gmat, gsize):
    return _dot((p * p).astype(BF16), gmat) * (1.0 / gsize)


def _norm_rope(p, gain, gmat, gsize, cos, sin):
    ms = _group_mean_sq(p, gmat, gsize)
    y = p * lax.rsqrt(ms + EPS) * gain
    return y * cos + _swap16(y) * sin


def _store_values_with_ones(v_ref, slab, v):
    low = lax.broadcasted_iota(jnp.int32, v.shape, 1) < HEAD_DIM
    v_ref[:, slab * LANES:(slab + 1) * LANES] = jnp.where(low, v, 1.0).astype(BF16)
    v_ref[:, (slab + 1) * LANES:(slab + 2) * LANES] = jnp.where(
        low, pltpu.roll(v, HEAD_DIM, 1), 1.0).astype(BF16)


def _conv3(main, prev_row, next_row, w):
    tm = main.shape[0]
    rid = lax.broadcasted_iota(jnp.int32, main.shape, 0)
    up = jnp.where(rid == 0, prev_row, pltpu.roll(main, 1, 0))
    dn = jnp.where(rid == tm - 1, next_row, pltpu.roll(main, tm - 1, 0))
    return up * w[0:1] + main * w[1:2] + dn * w[2:3]


def _inproj_kernel(x_ref, xprev_ref, xnext_ref, mod_ref, ng_ref, w_ref, aqn_ref, akn_ref, cqn_ref,
                   ckn_ref, g64_ref, g32_ref, cosa_ref, sina_ref, cosc_ref, sinc_ref,
                   hyw_ref, hyb_ref, scw_ref,
                   qa_ref, kat_ref, va_ref, ga_ref, gx0_ref, z_ref,
                   qc_ref, kct_ref, vc_ref, gc_ref, yd_ref, *, d_model, tps):
    tm = x_ref.shape[0]
    i = pl.program_id(0)
    has_prev = (i % tps != 0).astype(F32)
    has_next = (i % tps != tps - 1).astype(F32)
    x = jnp.concatenate([x_ref[...], xprev_ref[...], xnext_ref[...]], axis=0)
    prev_at, next_at = tm + SUBLANES - 1, tm + SUBLANES
    mod = mod_ref[0]
    shift = mod[:, 0:d_model]
    scale = mod[:, d_model:2 * d_model]
    ms = jnp.mean(x * x, axis=-1, keepdims=True)
    h = x * lax.rsqrt(ms + EPS) * ng_ref[...]
    hb_ext = (h * (1.0 + scale) + shift).astype(BF16)
    hb = hb_ext[0:tm]

    def proj(lo, width):
        return _dot(hb, w_ref[:, lo:lo + width])

    def proj_with_neighbours(lo, width):
        return _dot(hb_ext, w_ref[:, lo:lo + width])

    g64 = g64_ref[...]
    g32 = g32_ref[...]
    cosa, sina = cosa_ref[...], sina_ref[...]
    cosc, sinc = cosc_ref[...], sinc_ref[...]
    gw = GROUP_W
    sa = HEAD_DIM ** -0.5 * LOG2E
    sc = C_SUB ** -0.5 * LOG2E

    hd = HEAD_DIM

    def norm_rope_a(p, gain_ref):
        return _norm_rope(p, gain_ref[...], g64, HEAD_DIM, cosa, sina)

    def norm_rope_c(p, gain_ref):
        return _norm_rope(p, gain_ref[...], g32, C_SUB, cosc, sinc)

    def a_query(p):
        for i in range(gw // LANES):
            q = norm_rope_a(p[:, i * LANES:(i + 1) * LANES], aqn_ref)
            qa_ref[:, i * LANES:(i + 1) * LANES] = (q * sa).astype(BF16)

    def a_key_value(p):
        kt = norm_rope_a(p[:, 0:LANES], akn_ref).T.astype(BF16)
        kat_ref[0, 0, 0 * hd:1 * hd, :] = kt[0:hd]
        kat_ref[0, 0, 1 * hd:2 * hd, :] = kt[0:hd]
        kat_ref[0, 0, 2 * hd:3 * hd, :] = kt[hd:2 * hd]
        kat_ref[0, 0, 3 * hd:4 * hd, :] = kt[hd:2 * hd]
        _store_values_with_ones(va_ref, 0, p[:, LANES:2 * LANES])

    def a_gate(p):
        ga_ref[...] = p

    held = {}

    def b_conv(p):
        y = _conv3(p[0:tm], p[prev_at:prev_at + 1] * has_prev, p[next_at:next_at + 1] * has_next,
                   hyw_ref[...]) + hyb_ref[...]
        z_ref[...] = y[:, gw:2 * gw] * y[:, 2 * gw:3 * gw]
        held["x0"] = y[:, 0:gw]

    def b_gate(p):
        gx0_ref[...] = _silu(p) * held["x0"]

    def c_query(p):
        for i in range(gw // LANES):
            q = norm_rope_c(p[:, i * LANES:(i + 1) * LANES], cqn_ref)
            qc_ref[:, i * LANES:(i + 1) * LANES] = (q * sc).astype(BF16)

    def c_key(p):
        for i in range(gw // LANES):
            k = norm_rope_c(p[:, i * LANES:(i + 1) * LANES], ckn_ref)
            kct_ref[0, 0, i * LANES:(i + 1) * LANES, :] = k.T.astype(BF16)

    def c_value(p):
        for i in range(gw // LANES):
            _store_values_with_ones(vc_ref, 2 * i, p[:, i * LANES:(i + 1) * LANES])

    def c_gate(p):
        gc_ref[...] = p

    def d_conv(p):
        u = p[:, gw:2 * gw] * p[:, 2 * gw:3 * gw]
        cv = _conv3(u[0:tm], u[prev_at:prev_at + 1] * has_prev, u[next_at:next_at + 1] * has_next,
                    scw_ref[...])
        held["d"] = p[0:tm, 0:gw] * cv

    def d_gate(p):
        yd_ref[...] = (_silu(p) * held["d"]).astype(BF16)

    groups = [
        (gw, False, a_query), (gw, False, a_key_value), (gw, False, a_gate),
        (3 * gw, True, b_conv), (gw, False, b_gate),
        (gw, False, c_query), (gw, False, c_key), (gw, False, c_value), (gw, False, c_gate),
        (3 * gw, True, d_conv), (gw, False, d_gate),
    ]
    off = 0
    pending = None
    for width, ext, epilogue in groups:
        p = (proj_with_neighbours if ext else proj)(off, width)
        off += width
        if pending is not None:
            pending[0](pending[1])
        pending = (epilogue, p)
    pending[0](pending[1])


def _group_matrix(gsize):
    i = jnp.arange(LANES) // gsize
    return (i[:, None] == i[None, :]).astype(BF16)


def _halo_specs(tm, width, nrows):
    per = tm // SUBLANES
    last = nrows // SUBLANES - 1
    prev = pl.BlockSpec((SUBLANES, width), lambda i: (jnp.maximum(i * per - 1, 0), 0))
    nxt = pl.BlockSpec((SUBLANES, width), lambda i: (jnp.minimum((i + 1) * per, last), 0))
    return prev, nxt


def _inproj(x2, mod3, norm_g, w_in_bf, a_qn, a_kn, c_qn, c_kn, tables, hy_conv_w, hy_conv_b,
            sc_conv_w, *, seq, tm, layer):
    rows, d_model = x2.shape
    d_in = w_in_bf.shape[2]
    tps = seq // tm
    bsz = rows // seq
    gw = GROUP_W
    rep = lambda v, g: jnp.tile(v, LANES // g)[None, :]
    full = lambda r, c: pl.BlockSpec((r, c), lambda i: (0, 0))
    rowblk = lambda c: pl.BlockSpec((tm, c), lambda i: (i, 0))
    tab = pl.BlockSpec((tm, LANES), lambda i: (i % tps, 0))
    ktspec = lambda r: pl.BlockSpec((1, 1, r, tm), lambda i: (i // tps, i % tps, 0, 0))
    out_shape = [
        jax.ShapeDtypeStruct((rows, gw), BF16),
        jax.ShapeDtypeStruct((bsz, tps, 2 * LANES, tm), BF16),
        jax.ShapeDtypeStruct((rows, 2 * LANES), BF16),
        jax.ShapeDtypeStruct((rows, gw), F32),
        jax.ShapeDtypeStruct((rows, gw), F32),
        jax.ShapeDtypeStruct((rows, gw), F32),
        jax.ShapeDtypeStruct((rows, gw), BF16),
        jax.ShapeDtypeStruct((bsz, tps, gw, tm), BF16),
        jax.ShapeDtypeStruct((rows, 2 * gw), BF16),
        jax.ShapeDtypeStruct((rows, gw), F32),
        jax.ShapeDtypeStruct((rows, gw), BF16),
    ]
    out_specs = [
        rowblk(gw), ktspec(2 * LANES), rowblk(2 * LANES), rowblk(gw), rowblk(gw), rowblk(gw),
        rowblk(gw), ktspec(gw), rowblk(2 * gw), rowblk(gw), rowblk(gw),
    ]
    xprev, xnext = _halo_specs(tm, d_model, rows)
    return pl.pallas_call(
        functools.partial(_inproj_kernel, d_model=d_model, tps=tps),
        grid=(rows // tm,),
        in_specs=[
            rowblk(d_model), xprev, xnext,
            pl.BlockSpec((1, 1, 3 * d_model), lambda i: (i // tps, 0, 0)),
            full(1, d_model),
            pl.BlockSpec((None, d_model, d_in), lambda i: (layer, 0, 0)),
            full(1, LANES), full(1, LANES), full(1, LANES), full(1, LANES),
            full(LANES, LANES), full(LANES, LANES),
            tab, tab, tab, tab,
            full(3, 3 * gw), full(1, 3 * gw), full(3, gw),
        ],
        out_specs=out_specs,
        out_shape=out_shape,
        compiler_params=_params("arbitrary"),
        name="inproj",
    )(x2, x2, x2, mod3, norm_g[None, :], w_in_bf, rep(a_qn, HEAD_DIM), rep(a_kn, HEAD_DIM),
      rep(c_qn, C_SUB), rep(c_kn, C_SUB), _group_matrix(HEAD_DIM), _group_matrix(C_SUB), *tables,
      hy_conv_w, hy_conv_b[None, :], sc_conv_w)


def _stage(nc, kc, rows, qk=None, pv=None, mid=None):
    mrun = jnp.full((rows, LANES), NEG_BIG, F32)
    acc = jnp.zeros((rows, LANES), F32)
    for c in range(nc):
        if qk is not None:
            q2, kt_ref, krow, s_out = qk
            s = _dot(q2, kt_ref[0, c, krow:krow + LANES, :])
            s_out[c] = s
            for j in range(kc // LANES):
                mrun = jnp.maximum(mrun, s[:, j * LANES:(j + 1) * LANES])
        if pv is not None:
            s_in, mb, v_ref, vcol = pv
            s = s_in[c]
            pb = jnp.concatenate(
                [jnp.exp2(s[:, j * LANES:(j + 1) * LANES] - mb).astype(BF16)
                 for j in range(kc // LANES)], axis=1)
            acc = acc + _dot(pb, v_ref[c * kc:(c + 1) * kc, vcol:vcol + LANES])
        if mid is not None and c == nc // 2:
            mid()
    mb_new =jnp.broadcast_to(jnp.max(mrun, axis=-1, keepdims=True), (rows, LANES))
    return mb_new, acc


def _normalise(acc):
    return acc * (1.0 / pltpu.roll(acc, HEAD_DIM, 1))


def _init_pipeline(zeros=(), ones=()):
    @pl.when(pl.program_id(0) == 0)
    def _():
        for r in zeros:
            r[...] = jnp.zeros(r.shape, r.dtype)
        for r in ones:
            r[...] = jnp.ones(r.shape, r.dtype)


def _attn_a_kernel(q_ref, kt_ref, v_ref, vprev_ref, g0_ref, g1_ref, o0_ref, o1_ref,
                   s0_ref, s1_ref, mb_ref, acc_ref, *, nc, kc, tq):
    s_ref = (s0_ref, s1_ref)
    rows = 2 * tq
    low = lax.broadcasted_iota(jnp.int32, (tq, LANES), 1) < HEAD_DIM
    _init_pipeline(zeros=(s_ref[1], mb_ref), ones=(acc_ref,))

    def stack(h):
        qh = q_ref[:, h * LANES:(h + 1) * LANES]
        zero = jnp.zeros_like(qh)
        return jnp.concatenate([jnp.where(low, qh, zero), jnp.where(low, zero, qh)], axis=0)

    def finish(acc, g_ref):
        o = _normalise(acc)
        o = jnp.where(low, o[0:tq], pltpu.roll(o[tq:2 * tq], HEAD_DIM, 1))
        return (_silu(g_ref[...]) * o).astype(BF16)

    def finish_previous_0():
        o0_ref[...] = finish(acc_ref[...], g0_ref)

    mb0, acc1 = _stage(nc, kc, rows, qk=(stack(0), kt_ref, 0, s_ref[0]),
                       pv=(s_ref[1], mb_ref[...], vprev_ref, 0), mid=finish_previous_0)

    def finish_previous_1():
        o1_ref[...] = finish(acc1, g1_ref)

    mb1, acc0 = _stage(nc, kc, rows, qk=(stack(1), kt_ref, LANES, s_ref[1]),
                       pv=(s_ref[0], mb0, v_ref, 0), mid=finish_previous_1)
    mb_ref[...] = mb1
    acc_ref[...] = acc0


def _pipeline_specs(tq, seq, n_tiles, q_width, kt_rows, nc, kc, v_width, last_unit):
    nq = seq // tq
    cur = lambda i: jnp.minimum(i, n_tiles - 1)
    prev = lambda i: jnp.maximum(i - 1, 0)
    in_specs = [
        pl.BlockSpec((tq, q_width), lambda i: (cur(i), 0)),
        pl.BlockSpec((1, nc, kt_rows, kc), lambda i: (cur(i) // nq, 0, 0, 0)),
        pl.BlockSpec((seq, v_width), lambda i: (cur(i) // nq, 0)),
        pl.BlockSpec((seq, LANES), lambda i: (prev(i) // nq, last_unit)),
    ]
    tile_cur = lambda col: pl.BlockSpec((tq, LANES), lambda i: (cur(i), col))
    tile_prev = lambda col: pl.BlockSpec((tq, LANES), lambda i: (prev(i), col))
    return in_specs, tile_cur, tile_prev


def _attn_a(qa, kat, va, ga, *, seq, tq):
    rows = qa.shape[0]
    _, nc, _, kc = kat.shape
    n_tiles = rows // tq
    in_specs, _, tile_prev = _pipeline_specs(tq, seq, n_tiles, GROUP_W, 2 * LANES, nc, kc, 2 * LANES, 1)
    return pl.pallas_call(
        functools.partial(_attn_a_kernel, nc=nc, kc=kc, tq=tq),
        grid=(n_tiles + 1,),
        in_specs=in_specs + [tile_prev(0), tile_prev(1)],
        out_specs=[tile_prev(0), tile_prev(0)],
        out_shape=[jax.ShapeDtypeStruct((rows, LANES), BF16)] * 2,
        scratch_shapes=[pltpu.VMEM((nc, 2 * tq, kc), F32)] * 2 + [pltpu.VMEM((2 * tq, LANES), F32)] * 2,
        compiler_params=_params("arbitrary"),
        name="attn_a",
    )(qa, kat, va, va, ga, ga)


def _attn_c_kernel(q_ref, kt_ref, v_ref, vprev_ref, g0_ref, g1_ref, lam_ref, sub_ref, g64_ref,
                   o0_ref, o1_ref, s0_ref, s1_ref, mb_ref, acc2_ref,
                   *, nc, kc, tq, lambda_init):
    s_ref = (s0_ref, s1_ref, s0_ref, s1_ref)
    rows = 2 * tq
    lamv = lam_ref[...]
    lam = (jnp.exp(jnp.sum(lamv[0:1] * lamv[1:2], axis=-1, keepdims=True))
           - jnp.exp(jnp.sum(lamv[2:3] * lamv[3:4], axis=-1, keepdims=True)) + lambda_init)
    lane = lax.broadcasted_iota(jnp.int32, (tq, LANES), 1)
    low = lane < HEAD_DIM
    _init_pipeline(zeros=(s_ref[3], mb_ref), ones=(acc2_ref,))

    def stack(u):
        qh = q_ref[:, (u // 2) * LANES:(u // 2 + 1) * LANES]
        zero = jnp.zeros_like(qh)
        base = (u % 2) * HEAD_DIM
        in1 = (lane >= base) & (lane < base + C_SUB)
        in2 = (lane >= base + C_SUB) & (lane < base + HEAD_DIM)
        return jnp.concatenate([jnp.where(in1, qh, zero), jnp.where(in2, qh, zero)], axis=0)

    def qk(u):
        return (stack(u), kt_ref, (u // 2) * LANES, s_ref[u])

    def pv(u, mb):
        return (s_ref[u], mb, v_ref, u * LANES)

    def differential(acc):
        o = _normalise(acc)
        return o[0:tq] - lam * o[tq:2 * tq]

    def sub_norm(d_even, d_odd, g_ref):
        oi = jnp.where(low, d_even, pltpu.roll(d_odd, HEAD_DIM, 1))
        ms = _group_mean_sq(oi, g64_ref[...], HEAD_DIM)
        oc = oi * lax.rsqrt(ms + EPS) * sub_ref[...] * (1.0 - lambda_init)
        return (_silu(g_ref[...]) * oc).astype(BF16)

    mb0, acc3 = _stage(nc, kc, rows, qk=qk(0), pv=(s_ref[3], mb_ref[...], vprev_ref, 0))

    def finish_previous():
        o1_ref[...] = sub_norm(differential(acc2_ref[...]), differential(acc3), g1_ref)

    mb1, acc0 = _stage(nc, kc, rows, qk=qk(1), pv=pv(0, mb0), mid=finish_previous)
    mb2, acc1 = _stage(nc, kc, rows, qk=qk(2), pv=pv(1, mb1))

    def finish_current():
        o0_ref[...] = sub_norm(differential(acc0), differential(acc1), g0_ref)

    mb3, acc2 = _stage(nc, kc, rows, qk=qk(3), pv=pv(2, mb2), mid=finish_current)
    mb_ref[...] = mb3
    acc2_ref[...] = acc2


def _attn_c(qc, kct, vc, gc, lamv, c_subln, *, seq, tq, lambda_init):
    rows = qc.shape[0]
    _, nc, _, kc = kct.shape
    n_tiles = rows // tq
    full = lambda r, c: pl.BlockSpec((r, c), lambda i: (0, 0))
    in_specs, tile_cur, tile_prev = _pipeline_specs(
        tq, seq, n_tiles, GROUP_W, GROUP_W, nc, kc, 2 * GROUP_W, 3)
    return pl.pallas_call(
        functools.partial(_attn_c_kernel, nc=nc, kc=kc, tq=tq, lambda_init=lambda_init),
        grid=(n_tiles + 1,),
        in_specs=in_specs + [tile_cur(0), tile_prev(1),
                             full(4, C_SUB), full(1, LANES), full(LANES, LANES)],
        out_specs=[tile_cur(0), tile_prev(0)],
        out_shape=[jax.ShapeDtypeStruct((rows, LANES), BF16)] * 2,
        scratch_shapes=[pltpu.VMEM((nc, 2 * tq, kc), F32)] * 2
        + [pltpu.VMEM((2 * tq, LANES), F32)] * 2,
        compiler_params=_params("arbitrary"),
        name="attn_c",
    )(qc, kct, vc, vc, gc, gc, lamv, jnp.tile(c_subln, LANES // HEAD_DIM)[None, :], _group_matrix(HEAD_DIM))


HY_CH = 8


def _hyconv_kernel(z_ref, kk_ref, nrm_ref, o_ref, acc_ref, *, nb, bsz):
    for ch in range(z_ref.shape[0]):
        _hyconv_channel(z_ref.at[ch], kk_ref.at[ch], nrm_ref.at[ch], o_ref.at[ch], acc_ref.at[ch],
                        nb, bsz)


def _hyconv_channel(z_ref, kk_ref, nrm_ref, o_ref, acc_ref, nb, bsz):
    z = z_ref[...].reshape(nb * bsz, LANES)
    kk = kk_ref[...] * (1.0 / nrm_ref[...])
    offset = (lax.broadcasted_iota(jnp.int32, (LANES, LANES), 1)
              - lax.broadcasted_iota(jnp.int32, (LANES, LANES), 0)).astype(F32).astype(BF16)
    rolled = {}

    def circulant(m):
        if m not in rolled:
            rolled[m] = pltpu.roll(jnp.broadcast_to(kk[m:m + 1], (LANES, LANES)), 0, 1,
                                   stride=1, stride_axis=0).astype(BF16)
        return rolled[m]

    def toeplitz(d):
        return jnp.where(offset >= 0, circulant(d + nb), circulant(d + nb - 1))

    acc_ref[...] = jnp.zeros_like(acc_ref)
    for d in range(0, nb, 2):
        m = bsz * (nb - d)
        out = _dot(z[0:m].astype(BF16), jnp.concatenate([toeplitz(d), toeplitz(d + 1)], axis=1))
        acc_ref[bsz * d:bsz * d + m, :] += out[:, 0:LANES]
        acc_ref[bsz * (d + 1):bsz * d + m, :] += out[0:m - bsz, LANES:2 * LANES]
    for e in range(2, nb, 2):
        m = bsz * (nb - e)
        out = _dot(z[bsz * e:bsz * e + m].astype(BF16),
                   jnp.concatenate([toeplitz(-e), toeplitz(-e - 1)], axis=1))
        acc_ref[0:m, :] += out[:, 0:LANES]
        acc_ref[0:m - bsz, :] += out[bsz:m, LANES:2 * LANES]
    m = bsz * (nb - 1)
    acc_ref[0:m, :] += _dot(z[bsz:bsz + m].astype(BF16), toeplitz(-1))
    o_ref[...] = acc_ref[...].reshape(nb, bsz, LANES)


def _hyconv(zt, kk3, nrm3, layer):
    ch, nb, bsz, _ = zt.shape
    assert nb % 2 == 0 and ch % HY_CH == 0
    return pl.pallas_call(
        functools.partial(_hyconv_kernel, nb=nb, bsz=bsz),
        grid=(ch // HY_CH,),
        in_specs=[
            pl.BlockSpec((HY_CH, nb, bsz, LANES), lambda c: (c, 0, 0, 0)),
            pl.BlockSpec((None, HY_CH, 2 * nb, LANES), lambda c: (layer, c, 0, 0)),
            pl.BlockSpec((None, HY_CH, 1, LANES), lambda c: (layer, c, 0, 0)),
        ],
        out_specs=pl.BlockSpec((HY_CH, nb, bsz, LANES), lambda c: (c, 0, 0, 0)),
        out_shape=jax.ShapeDtypeStruct(zt.shape, F32),
        scratch_shapes=[pltpu.VMEM((HY_CH, nb * bsz, LANES), F32)],
        compiler_params=_params("arbitrary"),
        name="hyconv",
    )(zt, kk3, nrm3)


def _outproj_kernel(x_ref, mod_ref, ya0_ref, ya1_ref, gx0_ref, yb_ref, z_ref, hb_ref,
                    yc0_ref, yc1_ref, yd_ref, w_ref, o_ref, *, d_model):
    gate = mod_ref[0][:, 2 * d_model:3 * d_model]
    y_b = (gx0_ref[...] * (yb_ref[...] + hb_ref[...] * z_ref[...])).astype(BF16)
    cat = jnp.concatenate(
        [ya0_ref[...], ya1_ref[...], y_b, yc0_ref[...], yc1_ref[...], yd_ref[...]], axis=1)
    o_ref[...] = x_ref[...] + gate * _dot(cat, w_ref[...])


def _outproj(x2, mod3, ya, gx0, yb, z, hy_bias, yc, yd, w_out_bf, *, seq, tm, layer):
    rows, d_model = x2.shape
    tps = seq // tm
    gw = GROUP_W
    rowblk = lambda c: pl.BlockSpec((tm, c), lambda i: (i, 0))
    full = lambda r, c: pl.BlockSpec((r, c), lambda i: (0, 0))
    g = rowblk(gw)
    hf = rowblk(LANES)
    return pl.pallas_call(
        functools.partial(_outproj_kernel, d_model=d_model),
        grid=(rows // tm,),
        in_specs=[
            rowblk(d_model),
            pl.BlockSpec((1, 1, 3 * d_model), lambda i: (i // tps, 0, 0)),
            hf, hf, g, g, g, full(1, gw), hf, hf, g,
            pl.BlockSpec((None, w_out_bf.shape[1], d_model), lambda i: (layer, 0, 0)),
        ],
        out_specs=rowblk(d_model),
        out_shape=jax.ShapeDtypeStruct((rows, d_model), F32),
        compiler_params=_params("arbitrary"),
        name="outproj",
    )(x2, mod3, *ya, gx0, yb, z, hy_bias[None, :], *yc, yd, w_out_bf)


def _rope_tables(seq):
    def cos_sin(pos, dim):
        inv = ROPE_THETA ** (-jnp.arange(0, dim, 2, dtype=F32) / dim)
        ang = pos.astype(F32)[:, None] * inv[None, :]
        return jnp.cos(ang), jnp.sin(ang)

    t = jnp.arange(seq, dtype=jnp.int32)
    cr, sr = cos_sin(t // GRID_W, HEAD_DIM // 2)
    cc, sc = cos_sin(t % GRID_W, HEAD_DIM // 2)
    cq, sq = cos_sin(t, C_SUB)
    cosa = jnp.tile(jnp.concatenate([cr, cr, cc, cc], axis=1), (1, LANES // HEAD_DIM))
    sina = jnp.tile(jnp.concatenate([-sr, sr, -sc, sc], axis=1), (1, LANES // HEAD_DIM))
    cosc = jnp.tile(jnp.concatenate([cq, cq], axis=1), (1, LANES // C_SUB))
    sinc = jnp.tile(jnp.concatenate([-sq, sq], axis=1), (1, LANES // C_SUB))
    return cosa, sina, cosc, sinc


def _tile_rows(seq, want):
    return want if seq % want == 0 else seq


def kernel(x, c, norm_g, w_ada, b_ada, w_in, w_out, a_qn, a_kn, hy_conv_w, hy_conv_b, hy_w1, hy_b1,
           hy_freq, hy_w2, hy_b2, hy_w3, hy_bias, c_qn, c_kn, lam_q1, lam_k1, lam_q2, lam_k2,
           c_subln, sc_conv_w):
    bsz, seq, d_model = x.shape
    depth = w_in.shape[0]
    nb = seq // LANES
    tm = _tile_rows(seq, 512)
    tq = _tile_rows(seq, 256)
    tq_a = _tile_rows(seq, 512)

    tables = _rope_tables(seq)
    mod = _ada(c, w_ada, b_ada)
    kk, nrm = _hyfilter(seq, hy_w1, hy_b1, hy_freq, hy_w2, hy_b2, hy_w3)
    w_in_bf = w_in.astype(BF16)
    w_out_bf = w_out.astype(BF16)

    x2 = x.reshape(bsz * seq, d_model)
    for l in range(depth):
        mod3 = mod[l][:, None, :]
        (qa, kat, va, ga, gx0, z, qc, kct, vc, gc, yd) = _inproj(
            x2, mod3, norm_g[l], w_in_bf, a_qn[l], a_kn[l], c_qn[l], c_kn[l], tables,
            hy_conv_w[l], hy_conv_b[l], sc_conv_w[l], seq=seq, tm=tm, layer=l)
        ya = _attn_a(qa, kat, va, ga, seq=seq, tq=tq_a)
        lambda_init = 0.8 - 0.6 * math.exp(-0.3 * l)
        lamv = jnp.stack([lam_q1[l], lam_k1[l], lam_q2[l], lam_k2[l]])
        yc = _attn_c(qc, kct, vc, gc, lamv, c_subln[l], seq=seq, tq=tq_a, lambda_init=lambda_init)
        zt = z.reshape(bsz, nb, LANES, GROUP_W).transpose(3, 1, 0, 2)
        yt = _hyconv(zt, kk.reshape(depth, GROUP_W, 2 * nb, LANES), nrm[:, :, None, :], l)
        yb = yt.transpose(2, 1, 3, 0).reshape(bsz * seq, GROUP_W)
        x2 = _outproj(x2, mod3, ya, gx0, yb, z, hy_bias[l], yc, yd, w_out_bf, seq=seq, tm=tm,
                      layer=l)
    return x2.reshape(bsz, seq, d_model)
```
